```python
import math
import jax
import jax.numpy as jnp
from jax import lax
import numpy as np

D_MODEL = 2048
BATCH = 2
SEQ = 4096
DEPTH = 4
DEC_BATCH = 32
DEC_SEQ = 8
PAST_LEN = 16384
PAGE_SIZE = 128

D_A = D_MODEL // 2
HD_A = 64
N_Q_A = D_A // HD_A
N_KV_A = 4
GROUP_A = N_Q_A // N_KV_A
WINDOW = 128
D_B = D_MODEL // 2
DK_B = 128
DV_B = 128
H_B = D_B // DV_B
D_QKV_B = 2 * H_B * DK_B + H_B * DV_B
DN_CONV = 4
DN_CHUNK = 64
D_C = D_MODEL
CF_CONV = 31
N_EVEN = (DEPTH + 1) // 2
N_ODD = DEPTH // 2
D_IN_EVEN = N_Q_A * HD_A + 2 * N_KV_A * HD_A + D_A + D_QKV_B + 2 * H_B + D_B
D_IN_ODD = 3 * D_C
EPS = 1e-6

kernel_name = 'hybrid_swa_gdn_conformer_step'


def rms_norm(x, g):
    xf = x.astype(jnp.float32)
    y = xf * lax.rsqrt(jnp.mean(xf * xf, axis=-1, keepdims=True) + EPS)
    return (y * g.astype(jnp.float32)).astype(x.dtype)


def layer_norm(x, g, b):
    xf = x.astype(jnp.float32)
    xc = xf - jnp.mean(xf, axis=-1, keepdims=True)
    y = xc * lax.rsqrt(jnp.mean(xc * xc, axis=-1, keepdims=True) + EPS)
    return (y * g.astype(jnp.float32) + b.astype(jnp.float32)).astype(x.dtype)


def l2_normalize(x):
    return x * lax.rsqrt(jnp.sum(x * x, axis=-1, keepdims=True) + EPS)


def causal_depthwise_conv(x, prev, w):
    k = w.shape[0]
    xp = jnp.concatenate([prev.astype(x.dtype), x], axis=1)
    y = lax.conv_general_dilated(xp, w.astype(x.dtype)[:, None, :], window_strides=(1,),
                                 padding='VALID', dimension_numbers=('NWC', 'WIO', 'NWC'),
                                 feature_group_count=x.shape[-1])
    return y, xp[:, xp.shape[1] - (k - 1):]


def window_mask(qpos, kpos):
    d = qpos[..., :, None] - kpos[..., None, :]
    return (d >= 0) & (d <= WINDOW) & (kpos[..., None, :] >= 0)


def sink_attention(q, k, v, mask, sinks):
    s = jnp.einsum('...qhgd,...khd->...hgqk', q, k).astype(jnp.float32) * (HD_A ** -0.5)
    s = jnp.where(mask, s, -jnp.inf)
    sk = sinks.astype(jnp.float32)[:, :, None, None]
    m = jnp.maximum(jnp.max(s, axis=-1, keepdims=True), sk)
    e = jnp.exp(s - m)
    p = e / (jnp.sum(e, axis=-1, keepdims=True) + jnp.exp(sk - m))
    return jnp.einsum('...hgqk,...khd->...qhgd', p.astype(v.dtype), v)


def swa_banded(q, k, v, sinks):
    n, t = q.shape[:2]
    nb = t // WINDOW
    qb = q.reshape(n, nb, WINDOW, N_KV_A, GROUP_A, HD_A)
    kb = k.reshape(n, nb, WINDOW, N_KV_A, HD_A)
    vb = v.reshape(n, nb, WINDOW, N_KV_A, HD_A)

    def with_prev(z):
        prev = jnp.concatenate([jnp.zeros_like(z[:, :1]), z[:, :-1]], axis=1)
        return jnp.concatenate([prev, z], axis=2)

    blk = jnp.arange(nb)[:, None] * WINDOW
    qpos = blk + jnp.arange(WINDOW)[None, :]
    kpos = blk - WINDOW + jnp.arange(2 * WINDOW)[None, :]
    mask = window_mask(qpos, kpos)[:, None, None]
    o = sink_attention(qb, with_prev(kb), with_prev(vb), mask, sinks)
    return o.reshape(n, t, D_A)


def swa_with_buffer(q, k, v, k_buf, v_buf, sinks):
    n, t = q.shape[:2]
    buf = k_buf.shape[1]
    kk = jnp.concatenate([k_buf.astype(k.dtype), k], axis=1)
    vv = jnp.concatenate([v_buf.astype(v.dtype), v], axis=1)
    qpos = PAST_LEN + jnp.arange(t)
    kpos = PAST_LEN - buf + jnp.arange(buf + t)
    o = sink_attention(q, kk, vv, window_mask(qpos, kpos), sinks)
    return o.reshape(n, t, D_A), kk[:, t:], vv[:, t:]


def gated_delta_rule(q, k, v, g, beta, s0):
    f32 = jnp.float32
    n, t, h, _ = q.shape
    out_dtype = q.dtype
    q = l2_normalize(q.astype(f32)) * (DK_B ** -0.5)
    k = l2_normalize(k.astype(f32))
    v = v.astype(f32)
    c = min(DN_CHUNK, t)
    pad = (-t) % c
    nc = (t + pad) // c

    def chunks(z):
        z = jnp.pad(z, [(0, 0), (0, pad)] + [(0, 0)] * (z.ndim - 2))
        z = z.reshape((n, nc, c) + z.shape[2:])
        return jnp.moveaxis(jnp.moveaxis(z, 3, 2), 1, 0)

    qc, kc, vc, gc, bc = (chunks(z) for z in (q, k, v, g, beta))
    gcum = jnp.cumsum(gc, axis=-1)
    tril = jnp.tril(jnp.ones((c, c), dtype=bool))
    diff = gcum[..., :, None] - gcum[..., None, :]
    decay = jnp.where(tril, jnp.exp(jnp.where(tril, diff, 0.0)), 0.0)
    kb = kc * bc[..., None]
    lower = jnp.tril(jnp.einsum('...id,...jd->...ij', kb, kc) * decay, -1)
    a_mat = lower + jnp.eye(c, dtype=f32)
    rhs = jnp.concatenate([vc * bc[..., None], kb * jnp.exp(gcum)[..., None]], axis=-1)
    sol = lax.linalg.triangular_solve(a_mat, rhs, left_side=True, lower=True, unit_diagonal=True)
    u, w = sol[..., :DV_B], sol[..., DV_B:]
    attn = jnp.einsum('...id,...jd->...ij', qc, kc) * decay
    qg = qc * jnp.exp(gcum)[..., None]
    kg = kc * jnp.exp(gcum[..., -1:] - gcum)[..., None]
    g_last = jnp.exp(gcum[..., -1])

    def step(s, xs):
        u_i, w_i, attn_i, qg_i, kg_i, gl_i = xs
        v_new = u_i - jnp.einsum('nhck,nhkv->nhcv', w_i, s)
        o = jnp.einsum('nhck,nhkv->nhcv', qg_i, s) + jnp.einsum('nhij,nhjv->nhiv', attn_i, v_new)
        s = s * gl_i[..., None, None] + jnp.einsum('nhck,nhcv->nhkv', kg_i, v_new)
        return s, o

    s_fin, o = lax.scan(step, s0.astype(f32), (u, w, attn, qg, kg, g_last))
    o = jnp.moveaxis(jnp.moveaxis(o, 0, 1), 2, 3).reshape(n, nc * c, h, DV_B)[:, :t]
    return o.astype(out_dtype), s_fin.astype(s0.dtype)


def gated_deltanet_branch(qkv, a, b, z, conv_prev, s0, conv_w, a_log, dt_bias, norm_g):
    n, t = qkv.shape[:2]
    f32 = jnp.float32
    cqkv, conv_new = causal_depthwise_conv(qkv, conv_prev, conv_w)
    cqkv = jax.nn.silu(cqkv)
    q, k, v = jnp.split(cqkv, [H_B * DK_B, 2 * H_B * DK_B], axis=-1)
    g = -jnp.exp(a_log.astype(f32)) * jax.nn.softplus(a.astype(f32) + dt_bias.astype(f32))
    beta = jax.nn.sigmoid(b.astype(f32))
    o, s_new = gated_delta_rule(q.reshape(n, t, H_B, DK_B), k.reshape(n, t, H_B, DK_B),
                                v.reshape(n, t, H_B, DV_B), g, beta, s0)
    o = rms_norm(o, norm_g).reshape(n, t, D_B) * jax.nn.silu(z)
    return o, s_new, conv_new


def conformer_conv_branch(h, w_in, w_out, conv_prev, conv_w, conv_b, ln_g, ln_b):
    val, gate, z = jnp.split(h @ w_in, 3, axis=-1)
    u = val * jax.nn.sigmoid(gate)
    c, conv_new = causal_depthwise_conv(u, conv_prev, conv_w)
    c = layer_norm(c + conv_b.astype(c.dtype), ln_g, ln_b)
    y = jax.nn.silu(c) * jax.nn.silu(z)
    return y @ w_out, conv_new


def split_even(u):
    sizes = (N_Q_A * HD_A, N_KV_A * HD_A, N_KV_A * HD_A, D_A, D_QKV_B, H_B, H_B, D_B)
    idx = []
    acc = 0
    for s in sizes[:-1]:
        acc += s
        idx.append(acc)
    return jnp.split(u, idx, axis=-1)


def trunk(x, caches, norm_gain, w_in_even, w_out_even, attn_sinks, dn_conv_w, dn_a_log,
          dn_dt_bias, dn_norm_gain, w_in_odd, w_out_odd, cf_conv_w, cf_conv_b, cf_ln_gain,
          cf_ln_bias, final_norm_gain):
    n, t, _ = x.shape
    dtype = x.dtype
    new_k, new_v, new_s, new_dnc, new_cfc = [], [], [], [], []
    for layer in range(DEPTH):
        i = layer // 2
        h = rms_norm(x, norm_gain[layer])
        if layer % 2 == 0:
            qa, ka, va, za, qkv_b, a_b, b_b, z_b = split_even(h @ w_in_even[i])
            qa = qa.reshape(n, t, N_KV_A, GROUP_A, HD_A)
            ka = ka.reshape(n, t, N_KV_A, HD_A)
            va = va.reshape(n, t, N_KV_A, HD_A)
            sinks = attn_sinks[i].reshape(N_KV_A, GROUP_A)
            if caches is None:
                o_a = swa_banded(qa, ka, va, sinks)
                buf = min(WINDOW, t)
                k_buf, v_buf = ka[:, t - buf:], va[:, t - buf:]
                dn_prev = jnp.zeros((n, DN_CONV - 1, D_QKV_B), dtype)
                s0 = jnp.zeros((n, H_B, DK_B, DV_B), dtype)
            else:
                o_a, k_buf, v_buf = swa_with_buffer(qa, ka, va, caches[0][i], caches[1][i], sinks)
                s0 = caches[2][i]
                dn_prev = caches[3][i]
            o_a = o_a * jax.nn.silu(za)
            o_b, s_new, dnc_new = gated_deltanet_branch(qkv_b, a_b, b_b, z_b, dn_prev, s0,
                                                        dn_conv_w[i], dn_a_log[i],
                                                        dn_dt_bias[i], dn_norm_gain[i])
            y = jnp.concatenate([o_a, o_b], axis=-1) @ w_out_even[i]
            new_k.append(k_buf)
            new_v.append(v_buf)
            new_s.append(s_new)
            new_dnc.append(dnc_new)
        else:
            if caches is None:
                cf_prev = jnp.zeros((n, CF_CONV - 1, D_C), dtype)
            else:
                cf_prev = caches[4][i]
            y, cf_new = conformer_conv_branch(h, w_in_odd[i], w_out_odd[i], cf_prev, cf_conv_w[i],
                                              cf_conv_b[i], cf_ln_gain[i], cf_ln_bias[i])
            new_cfc.append(cf_new)
        x = x + y
    states = (jnp.stack(new_k), jnp.stack(new_v), jnp.stack(new_s), jnp.stack(new_dnc),
              jnp.stack(new_cfc))
    return rms_norm(x, final_norm_gain), states


def setup_inputs(seed: int = 0) -> dict:
    key = jax.random.key(seed)
    ks = jax.random.split(key, 24)
    f32 = jnp.float32

    def nrm(k, shape, scale):
        return jax.random.normal(k, shape, f32) * scale

    buf = min(WINDOW, PAST_LEN)
    dt = jnp.exp(jax.random.uniform(ks[12], (N_EVEN, H_B), f32, math.log(1e-3), math.log(1e-1)))
    return {
        'x_prompt': nrm(ks[0], (BATCH, SEQ, D_MODEL), 1.0),
        'x_sample': nrm(ks[1], (DEC_BATCH, DEC_SEQ, D_MODEL), 1.0),
        'cache_win_k': nrm(ks[2], (N_EVEN, DEC_BATCH, buf, N_KV_A, HD_A), 1.0),
        'cache_win_v': nrm(ks[3], (N_EVEN, DEC_BATCH, buf, N_KV_A, HD_A), 1.0),
        'state_dn': nrm(ks[4], (N_EVEN, DEC_BATCH, H_B, DK_B, DV_B), 0.1),
        'state_dn_conv': nrm(ks[5], (N_EVEN, DEC_BATCH, DN_CONV - 1, D_QKV_B), 1.0),
        'state_cf_conv': nrm(ks[6], (N_ODD, DEC_BATCH, CF_CONV - 1, D_C), 1.0),
        'norm_gain': 1.0 + nrm(ks[7], (DEPTH, D_MODEL), 0.02),
        'w_in_even': nrm(ks[8], (N_EVEN, D_MODEL, D_IN_EVEN), D_MODEL ** -0.5),
        'w_out_even': nrm(ks[9], (N_EVEN, D_A + D_B, D_MODEL), (D_A + D_B) ** -0.5),
        'attn_sinks': nrm(ks[10], (N_EVEN, N_Q_A), 0.5),
        'dn_conv_w': nrm(ks[11], (N_EVEN, DN_CONV, D_QKV_B), DN_CONV ** -0.5),
        'dn_a_log': jnp.log(jax.random.uniform(ks[13], (N_EVEN, H_B), f32, 1.0, 16.0)),
        'dn_dt_bias': dt + jnp.log(-jnp.expm1(-dt)),
        'dn_norm_gain': 1.0 + nrm(ks[14], (N_EVEN, DV_B), 0.02),
        'w_in_odd': nrm(ks[15], (N_ODD, D_MODEL, D_IN_ODD), D_MODEL ** -0.5),
        'w_out_odd': nrm(ks[16], (N_ODD, D_C, D_MODEL), D_C ** -0.5),
        'cf_conv_w': nrm(ks[17], (N_ODD, CF_CONV, D_C), CF_CONV ** -0.5),
        'cf_conv_b': nrm(ks[18], (N_ODD, D_C), 0.02),
        'cf_ln_gain': 1.0 + nrm(ks[19], (N_ODD, D_C), 0.02),
        'cf_ln_bias': nrm(ks[20], (N_ODD, D_C), 0.02),
        'final_norm_gain': 1.0 + nrm(ks[21], (D_MODEL,), 0.02),
    }


def reference(x_prompt, x_sample, cache_win_k, cache_win_v, state_dn, state_dn_conv,
              state_cf_conv, norm_gain, w_in_even, w_out_even, attn_sinks, dn_conv_w, dn_a_log,
              dn_dt_bias, dn_norm_gain, w_in_odd, w_out_odd, cf_conv_w, cf_conv_b, cf_ln_gain,
              cf_ln_bias, final_norm_gain):
    weights = (norm_gain, w_in_even, w_out_even, attn_sinks, dn_conv_w, dn_a_log, dn_dt_bias,
               dn_norm_gain, w_in_odd, w_out_odd, cf_conv_w, cf_conv_b, cf_ln_gain, cf_ln_bias,
               final_norm_gain)
    y_prompt, (p_k, p_v, p_dn, p_dnc, p_cfc) = trunk(x_prompt, None, *weights)
    y_sample, (s_k, s_v, s_dn, s_dnc, s_cfc) = trunk(
        x_sample, (cache_win_k, cache_win_v, state_dn, state_dn_conv, state_cf_conv), *weights)
    return (y_prompt, y_sample, p_k, p_v, p_dn, p_dnc, p_cfc, s_k, s_v, s_dn, s_dnc, s_cfc)
```

```python
import functools

import jax
import jax.numpy as jnp
from jax import lax
from jax.experimental import pallas as pl
from jax.experimental.pallas import tpu as pltpu

F32 = jnp.float32
BF16 = jnp.bfloat16

D_MODEL = 2048
DEPTH = 4
PAST_LEN = 16384
HD_A = 64
N_Q_A = 16
N_KV_A = 4
GROUP_A = 4
WINDOW = 128
D_A = 1024
D_KV_A = N_KV_A * HD_A
DK_B = 128
DV_B = 128
H_B = 8
D_B = 1024
D_QKV_B = 3072
DN_CONV = 4
DN_CHUNK = 64
D_C = 2048
CF_CONV = 31
EPS = 1e-6

COL_QA = 0
COL_ZA = 1024
COL_QKV_B = 2048
COL_ZB = 5120
COL_KA = 6144
COL_VA = 6400
COL_AB = 6656
N_EVEN_PAD = 6912
TN_EVEN = 768
TN_ODD = 512

VMEM_LIMIT = 56 * 1024 * 1024


def _cparams(*sem):
    return pltpu.CompilerParams(dimension_semantics=sem, vmem_limit_bytes=VMEM_LIMIT)


def _act_dtype(block_rows):
    return BF16 if block_rows % 16 == 0 else F32


def _sigmoid(x):
    return 1.0 / (1.0 + jnp.exp(-x))


def _silu(x):
    return x * _sigmoid(x)


def _dot(a, b):
    return jnp.dot(a.astype(BF16), b.astype(BF16), preferred_element_type=F32)


def _dot_nt(a, b):
    return lax.dot_general(a.astype(BF16), b.astype(BF16), (((1,), (1,)), ((), ())),
                           preferred_element_type=F32)


def _dot_tn(a, b):
    return lax.dot_general(a.astype(BF16), b.astype(BF16), (((0,), (0,)), ((), ())),
                           preferred_element_type=F32)


def _split(a):
    hi = a.astype(BF16)
    lo = (a - hi.astype(F32)).astype(BF16)
    return hi, lo


def _dot3(a, b):
    ah, al = _split(a)
    bh, bl = _split(b)
    d = functools.partial(jnp.dot, preferred_element_type=F32)
    return d(ah, bh) + (d(ah, bl) + d(al, bh))


NORM_ROWS = 256


def _normalize_rows(x_ref, g_ref, h_ref):
    tm = x_ref.shape[0]
    step = min(tm, NORM_ROWS)
    for r in range(0, tm, step):
        x = x_ref[r:r + step, :]
        ms = jnp.mean(x * x, axis=-1, keepdims=True)
        h_ref[r:r + step, :] = (x * lax.rsqrt(ms + EPS) * g_ref[...]).astype(BF16)


def _rms_mm_body(x_ref, g_ref, w_ref, o_ref, h_ref):
    @pl.when(pl.program_id(1) == 0)
    def _():
        _normalize_rows(x_ref, g_ref, h_ref)

    o_ref[...] = jnp.dot(h_ref[...], w_ref[...], preferred_element_type=F32)


def rms_matmul(x, g, w, tm, tn):
    m, k = x.shape
    n = w.shape[1]
    return pl.pallas_call(
        _rms_mm_body,
        grid=(m // tm, n // tn),
        in_specs=[
            pl.BlockSpec((tm, k), lambda i, j: (i, 0)),
            pl.BlockSpec((1, k), lambda i, j: (0, 0)),
            pl.BlockSpec((k, tn), lambda i, j: (0, j)),
        ],
        out_specs=pl.BlockSpec((tm, tn), lambda i, j: (i, j)),
        out_shape=jax.ShapeDtypeStruct((m, n), F32),
        scratch_shapes=[pltpu.VMEM((tm, k), BF16)],
        compiler_params=_cparams("parallel", "arbitrary"),
        name="rms_matmul",
    )(x, g.reshape(1, k), w)


def _rms_mm_glu_body(x_ref, g_ref, wv_ref, wg_ref, wz_ref, u_ref, sz_ref, h_ref):
    @pl.when(pl.program_id(1) == 0)
    def _():
        _normalize_rows(x_ref, g_ref, h_ref)

    h = h_ref[...]
    val = jnp.dot(h, wv_ref[...], preferred_element_type=F32)
    gate = jnp.dot(h, wg_ref[...], preferred_element_type=F32)
    z = jnp.dot(h, wz_ref[...], preferred_element_type=F32)
    u_ref[...] = val * _sigmoid(gate)
    sz_ref[...] = _silu(z)


def rms_matmul_glu(x, g, w, tm, tn):
    m, k = x.shape
    n = w.shape[1] // 3
    nt = n // tn
    return pl.pallas_call(
        _rms_mm_glu_body,
        grid=(m // tm, nt),
        in_specs=[
            pl.BlockSpec((tm, k), lambda i, j: (i, 0)),
            pl.BlockSpec((1, k), lambda i, j: (0, 0)),
            pl.BlockSpec((k, tn), lambda i, j: (0, j)),
            pl.BlockSpec((k, tn), lambda i, j: (0, j + nt)),
            pl.BlockSpec((k, tn), lambda i, j: (0, j + 2 * nt)),
        ],
        out_specs=[
            pl.BlockSpec((tm, tn), lambda i, j: (i, j)),
            pl.BlockSpec((tm, tn), lambda i, j: (i, j)),
        ],
        out_shape=[jax.ShapeDtypeStruct((m, n), F32), jax.ShapeDtypeStruct((m, n), F32)],
        scratch_shapes=[pltpu.VMEM((tm, k), BF16)],
        compiler_params=_cparams("parallel", "arbitrary"),
        name="rms_matmul_glu",
    )(x, g.reshape(1, k), w, w, w)


def _mm_res_body(*refs, n_in, final):
    a_refs = refs[:n_in]
    w_refs = refs[n_in:2 * n_in]
    res_ref = refs[2 * n_in]
    o_ref = refs[-1]
    acc = res_ref[...]
    for a_ref, w_ref in zip(a_refs, w_refs):
        acc = acc + jnp.dot(a_ref[...].astype(BF16), w_ref[...], preferred_element_type=F32)
    if final:
        g_ref = refs[2 * n_in + 1]
        ms = jnp.mean(acc * acc, axis=-1, keepdims=True)
        acc = acc * lax.rsqrt(ms + EPS) * g_ref[...]
    o_ref[...] = acc


def matmul_residual(acts, ws, res, tm, final_gain=None):
    m, n = res.shape
    n_in = len(acts)
    final = final_gain is not None
    in_specs = [pl.BlockSpec((tm, a.shape[1]), lambda i: (i, 0)) for a in acts]
    in_specs += [pl.BlockSpec(w.shape, lambda i: (0, 0)) for w in ws]
    in_specs += [pl.BlockSpec((tm, n), lambda i: (i, 0))]
    args = list(acts) + list(ws) + [res]
    if final:
        in_specs += [pl.BlockSpec((1, n), lambda i: (0, 0))]
        args += [final_gain.reshape(1, n)]
    return pl.pallas_call(
        functools.partial(_mm_res_body, n_in=n_in, final=final),
        grid=(m // tm,),
        in_specs=in_specs,
        out_specs=pl.BlockSpec((tm, n), lambda i: (i, 0)),
        out_shape=jax.ShapeDtypeStruct((m, n), F32),
        compiler_params=_cparams("parallel"),
        name="matmul_residual",
    )(*args)


def _attn_body(sink_ref, q_ref, za_ref, kp_ref, kc_ref, vp_ref, vc_ref, o_ref, *, tq, first_block_has_no_prev):
    q = q_ref[...]
    za = za_ref[...]

    def keys(prev_ref, cur_ref):
        cur = cur_ref[...]
        if tq < WINDOW:
            cur = jnp.concatenate([cur, jnp.zeros((WINDOW - tq, D_KV_A), F32)], axis=0)
        return jnp.concatenate([prev_ref[...], cur], axis=0).astype(BF16)

    k = keys(kp_ref, kc_ref)
    v = keys(vp_ref, vc_ref)
    rows = GROUP_A * tq
    r = lax.broadcasted_iota(jnp.int32, (rows, 2 * WINDOW), 0)
    c = lax.broadcasted_iota(jnp.int32, (rows, 2 * WINDOW), 1)
    i = r & (tq - 1)
    mask = (c >= i) & (c <= i + WINDOW)
    if first_block_has_no_prev:
        mask = mask & ((c >= WINDOW) | (pl.program_id(1) > 0))
    outs = []
    for h in range(N_KV_A):
        kh = k[:, h * HD_A:(h + 1) * HD_A]
        vh = v[:, h * HD_A:(h + 1) * HD_A]
        base = h * GROUP_A * HD_A
        qs = jnp.concatenate([q[:, base + g * HD_A: base + (g + 1) * HD_A] for g in range(GROUP_A)], axis=0)
        s = _dot_nt(qs, kh) * (HD_A ** -0.5)
        s = jnp.where(mask, s, -jnp.inf)
        sk = jnp.concatenate([jnp.full((tq, 1), sink_ref[h * GROUP_A + g], F32) for g in range(GROUP_A)], axis=0)
        m = jnp.maximum(jnp.max(s, axis=-1, keepdims=True), sk)
        e = jnp.exp(s - m)
        p = e / (jnp.sum(e, axis=-1, keepdims=True) + jnp.exp(sk - m))
        o = _dot(p, vh)
        outs += [o[g * tq:(g + 1) * tq] for g in range(GROUP_A)]
    o_all = jnp.concatenate(outs, axis=1)
    o_ref[...] = (o_all * _silu(za)).astype(o_ref.dtype)


def attention(u, sinks, prev_k, prev_v, prev_map, nseq, t, tq, first_block_has_no_prev):
    nb = t // tq
    row = lambda n, j: n * nb + j
    return pl.pallas_call(
        functools.partial(_attn_body, tq=tq, first_block_has_no_prev=first_block_has_no_prev),
        grid=(nseq, nb),
        in_specs=[
            pl.BlockSpec(memory_space=pltpu.SMEM),
            pl.BlockSpec((tq, D_A), lambda n, j: (row(n, j), COL_QA // D_A)),
            pl.BlockSpec((tq, D_A), lambda n, j: (row(n, j), COL_ZA // D_A)),
            pl.BlockSpec((WINDOW, D_KV_A), lambda n, j: prev_map(n, j, COL_KA // D_KV_A)),
            pl.BlockSpec((tq, D_KV_A), lambda n, j: (row(n, j), COL_KA // D_KV_A)),
            pl.BlockSpec((WINDOW, D_KV_A), lambda n, j: prev_map(n, j, COL_VA // D_KV_A)),
            pl.BlockSpec((tq, D_KV_A), lambda n, j: (row(n, j), COL_VA // D_KV_A)),
        ],
        out_specs=pl.BlockSpec((tq, D_A), lambda n, j: (row(n, j), 0)),
        out_shape=jax.ShapeDtypeStruct((nseq * t, D_A), _act_dtype(tq)),
        compiler_params=_cparams("parallel", "arbitrary"),
        name="swa_attention",
    )(sinks, u, u, prev_k, u, prev_v, u)


def _softplus(x):
    return jnp.maximum(x, 0.0) + jnp.log1p(jnp.exp(-jnp.abs(x)))


def _deltanet_body(q_ref, k_ref, v_ref, ab_ref, zb_ref, cprev_ref, s0_ref, cw_ref, alog_ref, dtb_ref, ng_ref,
                   o_ref, sfin_ref, xp_ref, s_ref, *, sb, tv, nc):
    c = pl.program_id(1)
    ch = DN_CHUNK
    halo = 8

    @pl.when(c == 0)
    def _():
        s_ref[...] = s0_ref[...]
        xp_ref[:, 0:halo, :] = cprev_ref[...]

    ii = lax.broadcasted_iota(jnp.int32, (ch, ch), 0)
    jj = lax.broadcasted_iota(jnp.int32, (ch, ch), 1)
    lower = ii >= jj
    strict = ii > jj
    eye = (ii == jj).astype(F32)
    tril_ones = lower.astype(F32)
    sel_rows = (lax.broadcasted_iota(jnp.int32, (H_B, 128), 0)
                == lax.broadcasted_iota(jnp.int32, (H_B, 128), 1)).astype(F32)
    row_id = lax.broadcasted_iota(jnp.int32, (ch, 1), 0)
    cw = cw_ref[...]
    hp = lax.Precision.HIGHEST

    for n in range(sb):
        xp_ref[n, halo:halo + tv, 0:1024] = q_ref[n]
        xp_ref[n, halo:halo + tv, 1024:2048] = k_ref[n]
        xp_ref[n, halo:halo + tv, 2048:3072] = v_ref[n]
        if tv < ch:
            xp_ref[n, halo + tv:halo + ch, :] = jnp.zeros((ch - tv, D_QKV_B), F32)
        y = jnp.zeros((ch, D_QKV_B), F32)
        for tap in range(DN_CONV):
            off = halo - (DN_CONV - 1) + tap
            y = y + cw[tap:tap + 1, :] * xp_ref[n, off:off + ch, :]
        y = _silu(y)
        if tv < ch:
            y = jnp.where(row_id < tv, y, 0.0)
        ab = ab_ref[n]
        if tv < ch:
            ab = jnp.concatenate([ab, jnp.zeros((ch - tv, 128), F32)], axis=0)
        gpre = -jnp.exp(alog_ref[...]) * _softplus(ab + dtb_ref[...])
        if tv < ch:
            gpre = jnp.where(row_id < tv, gpre, 0.0)
        beta_all = _sigmoid(ab)
        gcum = jnp.dot(tril_ones, gpre, precision=hp, preferred_element_type=F32)
        gcum_t = lax.dot_general(sel_rows, gcum, (((1,), (1,)), ((), ())), precision=hp,
                                 preferred_element_type=F32)
        zb = zb_ref[n]
        outs = []
        for h in range(H_B):
            qh = y[:, h * DK_B:(h + 1) * DK_B]
            kh = y[:, 1024 + h * DK_B:1024 + (h + 1) * DK_B]
            vh = y[:, 2048 + h * DV_B:2048 + (h + 1) * DV_B]
            qh = qh * lax.rsqrt(jnp.sum(qh * qh, axis=-1, keepdims=True) + EPS) * (DK_B ** -0.5)
            kh = kh * lax.rsqrt(jnp.sum(kh * kh, axis=-1, keepdims=True) + EPS)
            gc = gcum[:, h:h + 1]
            gr = gcum_t[h:h + 1, :]
            glast = gcum[tv - 1:tv, h:h + 1]
            beta = beta_all[:, H_B + h:H_B + h + 1]
            decay = jnp.where(lower, jnp.exp(jnp.where(lower, gc - gr, 0.0)), 0.0)
            egc = jnp.exp(gc)
            kbeta = kh * beta
            p_k = -jnp.where(strict, _dot_nt(kbeta, kh) * decay, 0.0)
            t_inv = eye + p_k
            for _ in range(5):
                p_k = _dot3(p_k, p_k)
                t_inv = t_inv + _dot3(t_inv, p_k)
            rhs = jnp.concatenate([vh * beta, kbeta * egc], axis=1)
            sol = _dot3(t_inv, rhs)
            u_i = sol[:, :DV_B]
            w_i = sol[:, DV_B:]
            attn = _dot_nt(qh, kh) * decay
            qg = qh * egc
            kg = kh * jnp.exp(glast - gc)
            s = s_ref[n, h]
            v_new = u_i - _dot(w_i, s)
            o = _dot(qg, s) + _dot(attn, v_new)
            s_ref[n, h] = s * jnp.exp(glast) + _dot_tn(kg, v_new)
            o = o * lax.rsqrt(jnp.mean(o * o, axis=-1, keepdims=True) + EPS) * ng_ref[...]
            outs.append(o)
        o_all = jnp.concatenate(outs, axis=1)[:tv]
        o_ref[n] = (o_all * _silu(zb)).astype(o_ref.dtype)
        xp_ref[n, 0:halo, :] = xp_ref[n, tv:tv + halo, :]

    @pl.when(c == nc - 1)
    def _():
        sfin_ref[...] = s_ref[...]


def deltanet(u3, conv_prev, s0, conv_w, a_log, dt_bias, norm_g, sb):
    nseq, t, _ = u3.shape
    tv = min(DN_CHUNK, t)
    nc = t // tv
    pad8 = lambda x: jnp.zeros((1, 128), F32).at[0, :H_B].set(x)
    blk = lambda col: pl.BlockSpec((sb, tv, 1024), lambda s, c: (s, c, col))
    const = lambda shape: pl.BlockSpec(shape, lambda s, c: tuple(0 for _ in shape))
    return pl.pallas_call(
        functools.partial(_deltanet_body, sb=sb, tv=tv, nc=nc),
        grid=(nseq // sb, nc),
        in_specs=[
            blk(COL_QKV_B // 1024), blk(COL_QKV_B // 1024 + 1), blk(COL_QKV_B // 1024 + 2),
            pl.BlockSpec((sb, tv, 128), lambda s, c: (s, c, COL_AB // 128)),
            blk(COL_ZB // 1024),
            pl.BlockSpec((sb, 8, D_QKV_B), lambda s, c: (s, 0, 0)),
            pl.BlockSpec((sb, H_B, DK_B, DV_B), lambda s, c: (s, 0, 0, 0)),
            const((DN_CONV, D_QKV_B)), const((1, 128)), const((1, 128)), const((1, DV_B)),
        ],
        out_specs=[
            pl.BlockSpec((sb, tv, D_B), lambda s, c: (s, c, 0)),
            pl.BlockSpec((sb, H_B, DK_B, DV_B), lambda s, c: (s, 0, 0, 0)),
        ],
        out_shape=[jax.ShapeDtypeStruct((nseq, t, D_B), _act_dtype(tv)),
                   jax.ShapeDtypeStruct((nseq, H_B, DK_B, DV_B), F32)],
        scratch_shapes=[pltpu.VMEM((sb, 8 + DN_CHUNK, D_QKV_B), F32),
                        pltpu.VMEM((sb, H_B, DK_B, DV_B), F32)],
        compiler_params=_cparams("parallel", "arbitrary"),
        name="gated_deltanet",
    )(u3, u3, u3, u3, u3, conv_prev, s0, conv_w, pad8(a_log), pad8(dt_bias), norm_g.reshape(1, DV_B))


CF_HALO = 32


def _cfconv_body(u_ref, sz_ref, prev_ref, cw_ref, cb_ref, lg_ref, lb_ref, o_ref, xp_ref, *, tt):
    @pl.when(pl.program_id(1) == 0)
    def _():
        xp_ref[0:CF_HALO, :] = prev_ref[0]

    xp_ref[CF_HALO:CF_HALO + tt, :] = u_ref[...]
    cw = cw_ref[...]
    acc = jnp.zeros((tt, D_C), F32)
    for tap in range(CF_CONV):
        off = CF_HALO - (CF_CONV - 1) + tap
        acc = acc + cw[tap:tap + 1, :] * xp_ref[off:off + tt, :]
    acc = acc + cb_ref[...]
    xc = acc - jnp.mean(acc, axis=-1, keepdims=True)
    y = xc * lax.rsqrt(jnp.mean(xc * xc, axis=-1, keepdims=True) + EPS)
    y = y * lg_ref[...] + lb_ref[...]
    o_ref[...] = (_silu(y) * sz_ref[...]).astype(o_ref.dtype)
    if tt >= CF_HALO:
        xp_ref[0:CF_HALO, :] = xp_ref[tt:tt + CF_HALO, :]


def conformer_conv(u, sz, prev, conv_w, conv_b, ln_g, ln_b, nseq, t, tt):
    nb = t // tt
    assert nb == 1 or tt >= CF_HALO
    row = lambda n, j: (n * nb + j, 0)
    const = lambda shape: pl.BlockSpec(shape, lambda n, j: (0, 0))
    return pl.pallas_call(
        functools.partial(_cfconv_body, tt=tt),
        grid=(nseq, nb),
        in_specs=[
            pl.BlockSpec((tt, D_C), row), pl.BlockSpec((tt, D_C), row),
            pl.BlockSpec((1, CF_HALO, D_C), lambda n, j: (n, 0, 0)),
            const((CF_CONV, D_C)), const((1, D_C)), const((1, D_C)), const((1, D_C)),
        ],
        out_specs=pl.BlockSpec((tt, D_C), row),
        out_shape=jax.ShapeDtypeStruct((nseq * t, D_C), _act_dtype(tt)),
        scratch_shapes=[pltpu.VMEM((CF_HALO + tt, D_C), F32)],
        compiler_params=_cparams("parallel", "arbitrary"),
        name="conformer_conv",
    )(u, sz, prev, conv_w, conv_b.reshape(1, D_C), ln_g.reshape(1, D_C), ln_b.reshape(1, D_C))


def _prep_even_weight(w):
    sl = lambda a, b: w[:, a:b]
    parts = [sl(0, 1024), sl(1536, 2560), sl(2560, 5632), sl(5648, 6672), sl(1024, 1280), sl(1280, 1536),
             sl(5632, 5648), jnp.zeros((D_MODEL, N_EVEN_PAD - 6672), w.dtype)]
    return jnp.concatenate(parts, axis=1).astype(BF16)


def _trunk(x, caches, weights, tm_in, tm_out, tq, dn_sb, tt):
    (norm_gain, w_in_even, w_out_even, attn_sinks, dn_conv_w, dn_a_log, dn_dt_bias, dn_norm_gain,
     w_in_odd, w_out_odd, cf_conv_w, cf_conv_b, cf_ln_gain, cf_ln_bias, final_norm_gain) = weights
    nseq, t, _ = x.shape
    m = nseq * t
    x = x.reshape(m, D_MODEL)
    new_k, new_v, new_s, new_dnc, new_cfc = [], [], [], [], []
    for layer in range(DEPTH):
        i = layer // 2
        last = layer == DEPTH - 1
        if layer % 2 == 0:
            u = rms_matmul(x, norm_gain[layer], w_in_even[i], tm_in, TN_EVEN)
            u3 = u.reshape(nseq, t, N_EVEN_PAD)
            ka = u3[:, :, COL_KA:COL_KA + D_KV_A]
            va = u3[:, :, COL_VA:COL_VA + D_KV_A]
            qkv_raw = u3[:, :, COL_QKV_B:COL_QKV_B + D_QKV_B]
            if caches is None:
                nb = t // tq
                prev_map = lambda n, j, col: (n * nb + jnp.maximum(j - 1, 0), col)
                o_a = attention(u, attn_sinks[i], u, u, prev_map, nseq, t, tq, True)
                k_buf, v_buf = ka[:, t - WINDOW:], va[:, t - WINDOW:]
                conv_prev = jnp.zeros((nseq, 8, D_QKV_B), F32)
                s0 = jnp.zeros((nseq, H_B, DK_B, DV_B), F32)
                dnc_new = qkv_raw[:, t - (DN_CONV - 1):]
            else:
                ck = caches[0][i].reshape(nseq * WINDOW, D_KV_A)
                cv = caches[1][i].reshape(nseq * WINDOW, D_KV_A)
                prev_map = lambda n, j, col: (n, 0)
                o_a = attention(u, attn_sinks[i], ck, cv, prev_map, nseq, t, tq, False)
                k_buf = jnp.concatenate([caches[0][i].reshape(nseq, WINDOW, D_KV_A), ka], axis=1)[:, t:]
                v_buf = jnp.concatenate([caches[1][i].reshape(nseq, WINDOW, D_KV_A), va], axis=1)[:, t:]
                s0 = caches[2][i]
                dn_prev = caches[3][i]
                conv_prev = jnp.concatenate([jnp.zeros((nseq, 8 - (DN_CONV - 1), D_QKV_B), F32), dn_prev], axis=1)
                dnc_new = jnp.concatenate([dn_prev, qkv_raw], axis=1)[:, t:]
            o_b, s_new = deltanet(u3, conv_prev, s0, dn_conv_w[i], dn_a_log[i], dn_dt_bias[i], dn_norm_gain[i], dn_sb)
            x = matmul_residual([o_a, o_b.reshape(m, D_B)], [w_out_even[i][:D_A], w_out_even[i][D_A:]], x, tm_out)
            new_k.append(k_buf.reshape(nseq, WINDOW, N_KV_A, HD_A))
            new_v.append(v_buf.reshape(nseq, WINDOW, N_KV_A, HD_A))
            new_s.append(s_new)
            new_dnc.append(dnc_new)
        else:
            uu, sz = rms_matmul_glu(x, norm_gain[layer], w_in_odd[i], tm_in, TN_ODD)
            uu3 = uu.reshape(nseq, t, D_C)
            if caches is None:
                prev = jnp.zeros((nseq, CF_HALO, D_C), F32)
                cf_new = uu3[:, t - (CF_CONV - 1):]
            else:
                cf_prev = caches[4][i]
                prev = jnp.concatenate([jnp.zeros((nseq, CF_HALO - (CF_CONV - 1), D_C), F32), cf_prev], axis=1)
                cf_new = jnp.concatenate([cf_prev, uu3], axis=1)[:, t:]
            y = conformer_conv(uu, sz, prev, cf_conv_w[i], cf_conv_b[i], cf_ln_gain[i], cf_ln_bias[i], nseq, t, tt)
            x = matmul_residual([y], [w_out_odd[i]], x, tm_out, final_norm_gain if last else None)
            new_cfc.append(cf_new)
    states = (jnp.stack(new_k), jnp.stack(new_v), jnp.stack(new_s), jnp.stack(new_dnc), jnp.stack(new_cfc))
    return x.reshape(nseq, t, D_MODEL), states


def kernel(x_prompt, x_sample, cache_win_k, cache_win_v, state_dn, state_dn_conv, state_cf_conv, norm_gain, w_in_even, w_out_even, attn_sinks, dn_conv_w, dn_a_log, dn_dt_bias, dn_norm_gain, w_in_odd, w_out_odd, cf_conv_w, cf_conv_b, cf_ln_gain, cf_ln_bias, final_norm_gain):
    n_even = w_in_even.shape[0]
    w_in_even_r = jnp.stack([_prep_even_weight(w_in_even[i]) for i in range(n_even)])
    weights = (norm_gain, w_in_even_r, w_out_even.astype(BF16), attn_sinks, dn_conv_w, dn_a_log, dn_dt_bias,
               dn_norm_gain, w_in_odd.astype(BF16), w_out_odd.astype(BF16), cf_conv_w, cf_conv_b, cf_ln_gain,
               cf_ln_bias, final_norm_gain)
    y_prompt, (p_k, p_v, p_dn, p_dnc, p_cfc) = _trunk(
        x_prompt, None, weights, tm_in=1024, tm_out=512, tq=WINDOW, dn_sb=2, tt=256)
    caches = (cache_win_k, cache_win_v, state_dn, state_dn_conv, state_cf_conv)
    y_sample, (s_k, s_v, s_dn, s_dnc, s_cfc) = _trunk(
        x_sample, caches, weights, tm_in=256, tm_out=256, tq=x_sample.shape[1], dn_sb=4, tt=x_sample.shape[1])
    return (y_prompt, y_sample, p_k, p_v, p_dn, p_dnc, p_cfc, s_k, s_v, s_dn, s_dnc, s_cfc)
```

```python
import functools

import jax
import jax.numpy as jnp
from jax import lax
from jax.experimental import pallas as pl
from jax.experimental.pallas import tpu as pltpu

F32 = jnp.float32
BF16 = jnp.bfloat16

D_MODEL = 2048
DEPTH = 4
PAST_LEN = 16384
HD_A = 64
N_Q_A = 16
N_KV_A = 4
GROUP_A = 4
WINDOW = 128
D_A = 1024
D_KV_A = N_KV_A * HD_A
DK_B = 128
DV_B = 128
H_B = 8
D_B = 1024
D_QKV_B = 3072
DN_CONV = 4
DN_CHUNK = 64
D_C = 2048
CF_CONV = 31
EPS = 1e-6

COL_QA = 0
COL_ZA = 1024
COL_QKV_B = 2048
COL_ZB = 5120
COL_KA = 6144
COL_VA = 6400
COL_AB = 6656
N_EVEN_PAD = 6912
TN_EVEN = 768
TN_ODD = 512

VMEM_LIMIT = 56 * 1024 * 1024


def _cparams(*sem):
    return pltpu.CompilerParams(dimension_semantics=sem, vmem_limit_bytes=VMEM_LIMIT)


def _act_dtype(block_rows):
    return BF16 if block_rows % 16 == 0 else F32


def _sigmoid(x):
    return 1.0 / (1.0 + jnp.exp(-x))


def _silu(x):
    return x * _sigmoid(x)


def _dot(a, b):
    return jnp.dot(a.astype(BF16), b.astype(BF16), preferred_element_type=F32)


def _dot_nt(a, b):
    return lax.dot_general(a.astype(BF16), b.astype(BF16), (((1,), (1,)), ((), ())),
                           preferred_element_type=F32)


def _dot_tn(a, b):
    return lax.dot_general(a.astype(BF16), b.astype(BF16), (((0,), (0,)), ((), ())),
                           preferred_element_type=F32)


def _split(a):
    hi = a.astype(BF16)
    lo = (a - hi.astype(F32)).astype(BF16)
    return hi, lo


NORM_ROWS = 256


def _normalize_rows(x_ref, g_ref, h_ref):
    tm = x_ref.shape[0]
    step = min(tm, NORM_ROWS)
    for r in range(0, tm, step):
        x = x_ref[r:r + step, :]
        ms = jnp.mean(x * x, axis=-1, keepdims=True)
        h_ref[r:r + step, :] = (x * lax.rsqrt(ms + EPS) * g_ref[...]).astype(BF16)


def _rms_mm_body(x_ref, g_ref, w_ref, o_ref, h_ref):
    @pl.when(pl.program_id(1) == 0)
    def _():
        _normalize_rows(x_ref, g_ref, h_ref)

    o_ref[...] = jnp.dot(h_ref[...], w_ref[...], preferred_element_type=F32)


def rms_matmul(x, g, w, tm, tn):
    m, k = x.shape
    n = w.shape[1]
    return pl.pallas_call(
        _rms_mm_body,
        grid=(m // tm, n // tn),
        in_specs=[
            pl.BlockSpec((tm, k), lambda i, j: (i, 0)),
            pl.BlockSpec((1, k), lambda i, j: (0, 0)),
            pl.BlockSpec((k, tn), lambda i, j: (0, j)),
        ],
        out_specs=pl.BlockSpec((tm, tn), lambda i, j: (i, j)),
        out_shape=jax.ShapeDtypeStruct((m, n), F32),
        scratch_shapes=[pltpu.VMEM((tm, k), BF16)],
        compiler_params=_cparams("parallel", "arbitrary"),
        name="rms_matmul",
    )(x, g.reshape(1, k), w)


def _rms_mm_glu_body(x_ref, g_ref, wv_ref, wg_ref, wz_ref, u_ref, sz_ref, h_ref):
    @pl.when(pl.program_id(1) == 0)
    def _():
        _normalize_rows(x_ref, g_ref, h_ref)

    h = h_ref[...]
    val = jnp.dot(h, wv_ref[...], preferred_element_type=F32)
    gate = jnp.dot(h, wg_ref[...], preferred_element_type=F32)
    z = jnp.dot(h, wz_ref[...], preferred_element_type=F32)
    u_ref[...] = val * _sigmoid(gate)
    sz_ref[...] = _silu(z)


def rms_matmul_glu(x, g, w, tm, tn):
    m, k = x.shape
    n = w.shape[1] // 3
    nt = n // tn
    return pl.pallas_call(
        _rms_mm_glu_body,
        grid=(m // tm, nt),
        in_specs=[
            pl.BlockSpec((tm, k), lambda i, j: (i, 0)),
            pl.BlockSpec((1, k), lambda i, j: (0, 0)),
            pl.BlockSpec((k, tn), lambda i, j: (0, j)),
            pl.BlockSpec((k, tn), lambda i, j: (0, j + nt)),
            pl.BlockSpec((k, tn), lambda i, j: (0, j + 2 * nt)),
        ],
        out_specs=[
            pl.BlockSpec((tm, tn), lambda i, j: (i, j)),
            pl.BlockSpec((tm, tn), lambda i, j: (i, j)),
        ],
        out_shape=[jax.ShapeDtypeStruct((m, n), F32), jax.ShapeDtypeStruct((m, n), F32)],
        scratch_shapes=[pltpu.VMEM((tm, k), BF16)],
        compiler_params=_cparams("parallel", "arbitrary"),
        name="rms_matmul_glu",
    )(x, g.reshape(1, k), w, w, w)


def _mm_res_body(*refs, n_in, final):
    a_refs = refs[:n_in]
    w_refs = refs[n_in:2 * n_in]
    res_ref = refs[2 * n_in]
    o_ref = refs[-1]
    acc = res_ref[...]
    for a_ref, w_ref in zip(a_refs, w_refs):
        acc = acc + jnp.dot(a_ref[...].astype(BF16), w_ref[...], preferred_element_type=F32)
    if final:
        g_ref = refs[2 * n_in + 1]
        ms = jnp.mean(acc * acc, axis=-1, keepdims=True)
        acc = acc * lax.rsqrt(ms + EPS) * g_ref[...]
    o_ref[...] = acc


def matmul_residual(acts, ws, res, tm, final_gain=None):
    m, n = res.shape
    n_in = len(acts)
    final = final_gain is not None
    in_specs = [pl.BlockSpec((tm, a.shape[1]), lambda i: (i, 0)) for a in acts]
    in_specs += [pl.BlockSpec(w.shape, lambda i: (0, 0)) for w in ws]
    in_specs += [pl.BlockSpec((tm, n), lambda i: (i, 0))]
    args = list(acts) + list(ws) + [res]
    if final:
        in_specs += [pl.BlockSpec((1, n), lambda i: (0, 0))]
        args += [final_gain.reshape(1, n)]
    return pl.pallas_call(
        functools.partial(_mm_res_body, n_in=n_in, final=final),
        grid=(m // tm,),
        in_specs=in_specs,
        out_specs=pl.BlockSpec((tm, n), lambda i: (i, 0)),
        out_shape=jax.ShapeDtypeStruct((m, n), F32),
        compiler_params=_cparams("parallel"),
        name="matmul_residual",
    )(*args)


def _attn_body(sink_ref, q_ref, za_ref, kp_ref, kc_ref, vp_ref, vc_ref, o_ref, *, tq, first_block_has_no_prev):
    q = q_ref[...]
    za = za_ref[...]

    def keys(prev_ref, cur_ref):
        cur = cur_ref[...]
        if tq < WINDOW:
            cur = jnp.concatenate([cur, jnp.zeros((WINDOW - tq, D_KV_A), F32)], axis=0)
        return jnp.concatenate([prev_ref[...], cur], axis=0).astype(BF16)

    k = keys(kp_ref, kc_ref)
    v = keys(vp_ref, vc_ref)
    rows = GROUP_A * tq
    r = lax.broadcasted_iota(jnp.int32, (rows, 2 * WINDOW), 0)
    c = lax.broadcasted_iota(jnp.int32, (rows, 2 * WINDOW), 1)
    i = r & (tq - 1)
    mask = (c >= i) & (c <= i + WINDOW)
    if first_block_has_no_prev:
        mask = mask & ((c >= WINDOW) | (pl.program_id(1) > 0))
    heads = range(N_KV_A)
    scores = []
    for h in heads:
        base = h * GROUP_A * HD_A
        qs = jnp.concatenate([q[:, base + g * HD_A: base + (g + 1) * HD_A] for g in range(GROUP_A)], axis=0)
        scores.append(_dot_nt(qs, k[:, h * HD_A:(h + 1) * HD_A]))
    probs = []
    for h in heads:
        s = jnp.where(mask, scores[h] * (HD_A ** -0.5), -jnp.inf)
        sk = jnp.concatenate([jnp.full((tq, 1), sink_ref[h * GROUP_A + g], F32) for g in range(GROUP_A)], axis=0)
        m = jnp.maximum(jnp.max(s, axis=-1, keepdims=True), sk)
        e = jnp.exp(s - m)
        probs.append(e / (jnp.sum(e, axis=-1, keepdims=True) + jnp.exp(sk - m)))
    outs = []
    for h in heads:
        o = _dot(probs[h], v[:, h * HD_A:(h + 1) * HD_A])
        outs += [o[g * tq:(g + 1) * tq] for g in range(GROUP_A)]
    o_all = jnp.concatenate(outs, axis=1)
    o_ref[...] = (o_all * _silu(za)).astype(o_ref.dtype)


def attention(u, sinks, prev_k, prev_v, prev_map, nseq, t, tq, first_block_has_no_prev):
    nb = t // tq
    row = lambda n, j: n * nb + j
    return pl.pallas_call(
        functools.partial(_attn_body, tq=tq, first_block_has_no_prev=first_block_has_no_prev),
        grid=(nseq, nb),
        in_specs=[
            pl.BlockSpec(memory_space=pltpu.SMEM),
            pl.BlockSpec((tq, D_A), lambda n, j: (row(n, j), COL_QA // D_A)),
            pl.BlockSpec((tq, D_A), lambda n, j: (row(n, j), COL_ZA // D_A)),
            pl.BlockSpec((WINDOW, D_KV_A), lambda n, j: prev_map(n, j, COL_KA // D_KV_A)),
            pl.BlockSpec((tq, D_KV_A), lambda n, j: (row(n, j), COL_KA // D_KV_A)),
            pl.BlockSpec((WINDOW, D_KV_A), lambda n, j: prev_map(n, j, COL_VA // D_KV_A)),
            pl.BlockSpec((tq, D_KV_A), lambda n, j: (row(n, j), COL_VA // D_KV_A)),
        ],
        out_specs=pl.BlockSpec((tq, D_A), lambda n, j: (row(n, j), 0)),
        out_shape=jax.ShapeDtypeStruct((nseq * t, D_A), _act_dtype(tq)),
        compiler_params=_cparams("parallel", "arbitrary"),
        name="swa_attention",
    )(sinks, u, u, prev_k, u, prev_v, u)


def _softplus(x):
    return jnp.maximum(x, 0.0) + jnp.log1p(jnp.exp(-jnp.abs(x)))


def _deltanet_body(q_ref, k_ref, v_ref, ab_ref, zb_ref, cprev_ref, s0_ref, cw_ref, alog_ref, dtb_ref, ng_ref,
                   o_ref, sfin_ref, xp_ref, s_ref, *, sb, tv, nc):
    c = pl.program_id(1)
    ch = DN_CHUNK
    halo = 8

    @pl.when(c == 0)
    def _():
        s_ref[...] = s0_ref[...]
        xp_ref[:, 0:halo, :] = cprev_ref[...]

    ii = lax.broadcasted_iota(jnp.int32, (ch, ch), 0)
    jj = lax.broadcasted_iota(jnp.int32, (ch, ch), 1)
    lower = ii >= jj
    strict = ii > jj
    eye = (ii == jj).astype(F32)
    tril_ones = lower.astype(F32)
    sel_rows = (lax.broadcasted_iota(jnp.int32, (H_B, 128), 0)
                == lax.broadcasted_iota(jnp.int32, (H_B, 128), 1)).astype(F32)
    row_id = lax.broadcasted_iota(jnp.int32, (ch, 1), 0)
    cw = cw_ref[...]
    hp = lax.Precision.HIGHEST

    units = []
    for n in range(sb):
        xp_ref[n, halo:halo + tv, 0:1024] = q_ref[n]
        xp_ref[n, halo:halo + tv, 1024:2048] = k_ref[n]
        xp_ref[n, halo:halo + tv, 2048:3072] = v_ref[n]
        if tv < ch:
            xp_ref[n, halo + tv:halo + ch, :] = jnp.zeros((ch - tv, D_QKV_B), F32)
        y = jnp.zeros((ch, D_QKV_B), F32)
        for tap in range(DN_CONV):
            off = halo - (DN_CONV - 1) + tap
            y = y + cw[tap:tap + 1, :] * xp_ref[n, off:off + ch, :]
        y = _silu(y)
        if tv < ch:
            y = jnp.where(row_id < tv, y, 0.0)
        xp_ref[n, 0:halo, :] = xp_ref[n, tv:tv + halo, :]
        ab = ab_ref[n]
        if tv < ch:
            ab = jnp.concatenate([ab, jnp.zeros((ch - tv, 128), F32)], axis=0)
        gpre = -jnp.exp(alog_ref[...]) * _softplus(ab + dtb_ref[...])
        if tv < ch:
            gpre = jnp.where(row_id < tv, gpre, 0.0)
        beta_all = _sigmoid(ab)
        gcum = jnp.dot(tril_ones, gpre, precision=hp, preferred_element_type=F32)
        gcum_t = lax.dot_general(sel_rows, gcum, (((1,), (1,)), ((), ())), precision=hp,
                                 preferred_element_type=F32)
        for h in range(H_B):
            qh = y[:, h * DK_B:(h + 1) * DK_B]
            kh = y[:, 1024 + h * DK_B:1024 + (h + 1) * DK_B]
            vh = y[:, 2048 + h * DV_B:2048 + (h + 1) * DV_B]
            qh = qh * lax.rsqrt(jnp.sum(qh * qh, axis=-1, keepdims=True) + EPS) * (DK_B ** -0.5)
            kh = kh * lax.rsqrt(jnp.sum(kh * kh, axis=-1, keepdims=True) + EPS)
            gc = gcum[:, h:h + 1]
            gr = gcum_t[h:h + 1, :]
            glast = gcum[tv - 1:tv, h:h + 1]
            beta = beta_all[:, H_B + h:H_B + h + 1]
            egc = jnp.exp(gc)
            kbeta = kh * beta
            units.append(dict(
                n=n, h=h, qh=qh, kh=kh, kbeta=kbeta, glast=glast,
                decay=jnp.where(lower, jnp.exp(jnp.where(lower, gc - gr, 0.0)), 0.0),
                rhs=jnp.concatenate([vh * beta, kbeta * egc], axis=1),
                qg=qh * egc, kg=kh * jnp.exp(glast - gc)))

    for u in units:
        kq = _dot_nt(jnp.concatenate([u["kbeta"], u["qh"]], axis=0), u["kh"])
        u["w"] = jnp.concatenate([-jnp.where(strict, kq[:ch] * u["decay"], 0.0), eye], axis=1)
        u["attn"] = kq[ch:] * u["decay"]

    right = lax.broadcasted_iota(jnp.int32, (ch, 2 * ch), 1) >= ch
    for _ in range(max(1, (tv - 1).bit_length())):
        prods = [_dot(u["w"][:, :ch], u["w"]) for u in units]
        for u, m in zip(units, prods):
            u["w"] = m + jnp.where(right, u["w"], 0.0)

    for u in units:
        hi, lo = _split(u["rhs"])
        sol = jnp.dot(u["w"][:, ch:].astype(BF16), jnp.concatenate([hi, lo], axis=1), preferred_element_type=F32)
        u["u"] = sol[:, 0:DV_B] + sol[:, 2 * DV_B:3 * DV_B]
        u["wk"] = sol[:, DV_B:2 * DV_B] + sol[:, 3 * DV_B:4 * DV_B]

    for u in units:
        ws_qs = _dot(jnp.concatenate([u["wk"], u["qg"]], axis=0), s_ref[u["n"], u["h"]])
        u["v_new"] = u["u"] - ws_qs[:ch]
        u["o"] = ws_qs[ch:]

    for u in units:
        u["o"] = u["o"] + _dot(u["attn"], u["v_new"])
        s_ref[u["n"], u["h"]] = s_ref[u["n"], u["h"]] * jnp.exp(u["glast"]) + _dot_tn(u["kg"], u["v_new"])

    for n in range(sb):
        outs = []
        for u in units[n * H_B:(n + 1) * H_B]:
            o = u["o"]
            outs.append(o * lax.rsqrt(jnp.mean(o * o, axis=-1, keepdims=True) + EPS) * ng_ref[...])
        o_all = jnp.concatenate(outs, axis=1)[:tv]
        o_ref[n] = (o_all * _silu(zb_ref[n])).astype(o_ref.dtype)

    @pl.when(c == nc - 1)
    def _():
        sfin_ref[...] = s_ref[...]


def deltanet(u3, conv_prev, s0, conv_w, a_log, dt_bias, norm_g, sb):
    nseq, t, _ = u3.shape
    tv = min(DN_CHUNK, t)
    nc = t // tv
    pad8 = lambda x: jnp.zeros((1, 128), F32).at[0, :H_B].set(x)
    blk = lambda col: pl.BlockSpec((sb, tv, 1024), lambda s, c: (s, c, col))
    const = lambda shape: pl.BlockSpec(shape, lambda s, c: tuple(0 for _ in shape))
    return pl.pallas_call(
        functools.partial(_deltanet_body, sb=sb, tv=tv, nc=nc),
        grid=(nseq // sb, nc),
        in_specs=[
            blk(COL_QKV_B // 1024), blk(COL_QKV_B // 1024 + 1), blk(COL_QKV_B // 1024 + 2),
            pl.BlockSpec((sb, tv, 128), lambda s, c: (s, c, COL_AB // 128)),
            blk(COL_ZB // 1024),
            pl.BlockSpec((sb, 8, D_QKV_B), lambda s, c: (s, 0, 0)),
            pl.BlockSpec((sb, H_B, DK_B, DV_B), lambda s, c: (s, 0, 0, 0)),
            const((DN_CONV, D_QKV_B)), const((1, 128)), const((1, 128)), const((1, DV_B)),
        ],
        out_specs=[
            pl.BlockSpec((sb, tv, D_B), lambda s, c: (s, c, 0)),
            pl.BlockSpec((sb, H_B, DK_B, DV_B), lambda s, c: (s, 0, 0, 0)),
        ],
        out_shape=[jax.ShapeDtypeStruct((nseq, t, D_B), _act_dtype(tv)),
                   jax.ShapeDtypeStruct((nseq, H_B, DK_B, DV_B), F32)],
        scratch_shapes=[pltpu.VMEM((sb, 8 + DN_CHUNK, D_QKV_B), F32),
                        pltpu.VMEM((sb, H_B, DK_B, DV_B), F32)],
        compiler_params=_cparams("parallel", "arbitrary"),
        name="gated_deltanet",
    )(u3, u3, u3, u3, u3, conv_prev, s0, conv_w, pad8(a_log), pad8(dt_bias), norm_g.reshape(1, DV_B))


CF_HALO = 32


def _cfconv_body(u_ref, sz_ref, prev_ref, cw_ref, cb_ref, lg_ref, lb_ref, o_ref, xp_ref, *, tt):
    @pl.when(pl.program_id(1) == 0)
    def _():
        xp_ref[0:CF_HALO, :] = prev_ref[0]

    xp_ref[CF_HALO:CF_HALO + tt, :] = u_ref[...]
    cw = cw_ref[...]
    first = CF_HALO - (CF_CONV - 1)
    acc = cb_ref[...]
    for b in range(8):
        rows = tt if b == 0 else tt + 8
        z = None
        for a in range((CF_HALO + 8) // 8):
            tap = 8 * a + b - first
            if 0 <= tap < CF_CONV:
                term = cw[tap:tap + 1, :] * xp_ref[8 * a:8 * a + rows, :]
                z = term if z is None else z + term
        acc = acc + (z if b == 0 else z[b:b + tt])
    xc = acc - jnp.mean(acc, axis=-1, keepdims=True)
    y = xc * lax.rsqrt(jnp.mean(xc * xc, axis=-1, keepdims=True) + EPS)
    y = y * lg_ref[...] + lb_ref[...]
    o_ref[...] = (_silu(y) * sz_ref[...]).astype(o_ref.dtype)
    if tt >= CF_HALO:
        xp_ref[0:CF_HALO, :] = xp_ref[tt:tt + CF_HALO, :]


def conformer_conv(u, sz, prev, conv_w, conv_b, ln_g, ln_b, nseq, t, tt):
    nb = t // tt
    assert nb == 1 or tt >= CF_HALO
    row = lambda n, j: (n * nb + j, 0)
    const = lambda shape: pl.BlockSpec(shape, lambda n, j: (0, 0))
    return pl.pallas_call(
        functools.partial(_cfconv_body, tt=tt),
        grid=(nseq, nb),
        in_specs=[
            pl.BlockSpec((tt, D_C), row), pl.BlockSpec((tt, D_C), row),
            pl.BlockSpec((1, CF_HALO, D_C), lambda n, j: (n, 0, 0)),
            const((CF_CONV, D_C)), const((1, D_C)), const((1, D_C)), const((1, D_C)),
        ],
        out_specs=pl.BlockSpec((tt, D_C), row),
        out_shape=jax.ShapeDtypeStruct((nseq * t, D_C), _act_dtype(tt)),
        scratch_shapes=[pltpu.VMEM((CF_HALO + tt, D_C), F32)],
        compiler_params=_cparams("parallel", "arbitrary"),
        name="conformer_conv",
    )(u, sz, prev, conv_w, conv_b.reshape(1, D_C), ln_g.reshape(1, D_C), ln_b.reshape(1, D_C))


def _prep_even_weight(w):
    sl = lambda a, b: w[:, a:b]
    parts = [sl(0, 1024), sl(1536, 2560), sl(2560, 5632), sl(5648, 6672), sl(1024, 1280), sl(1280, 1536),
             sl(5632, 5648), jnp.zeros((D_MODEL, N_EVEN_PAD - 6672), w.dtype)]
    return jnp.concatenate(parts, axis=1).astype(BF16)


def _trunk(x, caches, weights, tm_in, tm_out, tq, dn_sb, tt):
    (norm_gain, w_in_even, w_out_even, attn_sinks, dn_conv_w, dn_a_log, dn_dt_bias, dn_norm_gain,
     w_in_odd, w_out_odd, cf_conv_w, cf_conv_b, cf_ln_gain, cf_ln_bias, final_norm_gain) = weights
    nseq, t, _ = x.shape
    m = nseq * t
    x = x.reshape(m, D_MODEL)
    new_k, new_v, new_s, new_dnc, new_cfc = [], [], [], [], []
    for layer in range(DEPTH):
        i = layer // 2
        last = layer == DEPTH - 1
        if layer % 2 == 0:
            u = rms_matmul(x, norm_gain[layer], w_in_even[i], tm_in, TN_EVEN)
            u3 = u.reshape(nseq, t, N_EVEN_PAD)
            ka = u3[:, :, COL_KA:COL_KA + D_KV_A]
            va = u3[:, :, COL_VA:COL_VA + D_KV_A]
            qkv_raw = u3[:, :, COL_QKV_B:COL_QKV_B + D_QKV_B]
            if caches is None:
                nb = t // tq
                prev_map = lambda n, j, col: (n * nb + jnp.maximum(j - 1, 0), col)
                o_a = attention(u, attn_sinks[i], u, u, prev_map, nseq, t, tq, True)
                k_buf, v_buf = ka[:, t - WINDOW:], va[:, t - WINDOW:]
                conv_prev = jnp.zeros((nseq, 8, D_QKV_B), F32)
                s0 = jnp.zeros((nseq, H_B, DK_B, DV_B), F32)
                dnc_new = qkv_raw[:, t - (DN_CONV - 1):]
            else:
                ck = caches[0][i].reshape(nseq * WINDOW, D_KV_A)
                cv = caches[1][i].reshape(nseq * WINDOW, D_KV_A)
                prev_map = lambda n, j, col: (n, 0)
                o_a = attention(u, attn_sinks[i], ck, cv, prev_map, nseq, t, tq, False)
                k_buf = jnp.concatenate([caches[0][i].reshape(nseq, WINDOW, D_KV_A), ka], axis=1)[:, t:]
                v_buf = jnp.concatenate([caches[1][i].reshape(nseq, WINDOW, D_KV_A), va], axis=1)[:, t:]
                s0 = caches[2][i]
                dn_prev = caches[3][i]
                conv_prev = jnp.concatenate([jnp.zeros((nseq, 8 - (DN_CONV - 1), D_QKV_B), F32), dn_prev], axis=1)
                dnc_new = jnp.concatenate([dn_prev, qkv_raw], axis=1)[:, t:]
            o_b, s_new = deltanet(u3, conv_prev, s0, dn_conv_w[i], dn_a_log[i], dn_dt_bias[i], dn_norm_gain[i], dn_sb)
            x = matmul_residual([o_a, o_b.reshape(m, D_B)], [w_out_even[i][:D_A], w_out_even[i][D_A:]], x, tm_out)
            new_k.append(k_buf.reshape(nseq, WINDOW, N_KV_A, HD_A))
            new_v.append(v_buf.reshape(nseq, WINDOW, N_KV_A, HD_A))
            new_s.append(s_new)
            new_dnc.append(dnc_new)
        else:
            uu, sz = rms_matmul_glu(x, norm_gain[layer], w_in_odd[i], tm_in, TN_ODD)
            uu3 = uu.reshape(nseq, t, D_C)
            if caches is None:
                prev = jnp.zeros((nseq, CF_HALO, D_C), F32)
                cf_new = uu3[:, t - (CF_CONV - 1):]
            else:
                cf_prev = caches[4][i]
                prev = jnp.concatenate([jnp.zeros((nseq, CF_HALO - (CF_CONV - 1), D_C), F32), cf_prev], axis=1)
                cf_new = jnp.concatenate([cf_prev, uu3], axis=1)[:, t:]
            y = conformer_conv(uu, sz, prev, cf_conv_w[i], cf_conv_b[i], cf_ln_gain[i], cf_ln_bias[i], nseq, t, tt)
            x = matmul_residual([y], [w_out_odd[i]], x, tm_out, final_norm_gain if last else None)
            new_cfc.append(cf_new)
    states = (jnp.stack(new_k), jnp.stack(new_v), jnp.stack(new_s), jnp.stack(new_dnc), jnp.stack(new_cfc))
    return x.reshape(nseq, t, D_MODEL), states


def kernel(x_prompt, x_sample, cache_win_k, cache_win_v, state_dn, state_dn_conv, state_cf_conv, norm_gain, w_in_even, w_out_even, attn_sinks, dn_conv_w, dn_a_log, dn_dt_bias, dn_norm_gain, w_in_odd, w_out_odd, cf_conv_w, cf_conv_b, cf_ln_gain, cf_ln_bias, final_norm_gain):
    n_even = w_in_even.shape[0]
    w_in_even_r = jnp.stack([_prep_even_weight(w_in_even[i]) for i in range(n_even)])
    weights = (norm_gain, w_in_even_r, w_out_even.astype(BF16), attn_sinks, dn_conv_w, dn_a_log, dn_dt_bias,
               dn_norm_gain, w_in_odd.astype(BF16), w_out_odd.astype(BF16), cf_conv_w, cf_conv_b, cf_ln_gain,
               cf_ln_bias, final_norm_gain)
    y_prompt, (p_k, p_v, p_dn, p_dnc, p_cfc) = _trunk(
        x_prompt, None, weights, tm_in=1024, tm_out=512, tq=WINDOW, dn_sb=2, tt=256)
    caches = (cache_win_k, cache_win_v, state_dn, state_dn_conv, state_cf_conv)
    y_sample, (s_k, s_v, s_dn, s_dnc, s_cfc) = _trunk(
        x_sample, caches, weights, tm_in=256, tm_out=256, tq=x_sample.shape[1], dn_sb=4, tt=x_sample.shape[1])
    return (y_prompt, y_sample, p_k, p_v, p_dn, p_dnc, p_cfc, s_k, s_v, s_dn, s_dnc, s_cfc)
```

```python
import functools

import jax
import jax.numpy as jnp
from jax import lax
from jax.experimental import pallas as pl
from jax.experimental.pallas import tpu as pltpu

F32 = jnp.float32
BF16 = jnp.bfloat16

D_MODEL = 2048
DEPTH = 4
HD_A = 64
N_KV_A = 4
GROUP_A = 4
WINDOW = 128
D_A = 1024
D_KV_A = N_KV_A * HD_A
DK_B = 128
DV_B = 128
H_B = 8
D_B = 1024
D_QKV_B = 3072
DN_CONV = 4
DN_CHUNK = 64
D_C = 2048
CF_CONV = 31
EPS = 1e-6

COL_QA = 0
COL_ZA = 1024
COL_QKV_B = 2048
COL_ZB = 5120
COL_KA = 6144
COL_VA = 6400
COL_AB = 6656
N_EVEN = 6672
N_EVEN_PAD = 6912
TN_EVEN = 768
TN_ODD = 512

LANES = 128
SUBLANES = 8
VMEM_LIMIT = 56 * 1024 * 1024


def _cparams(*sem):
    return pltpu.CompilerParams(dimension_semantics=sem, vmem_limit_bytes=VMEM_LIMIT)


def _act_dtype(block_rows):
    return BF16 if block_rows % 16 == 0 else F32


def _sigmoid(x):
    return 1.0 / (1.0 + jnp.exp(-x))


def _silu(x):
    return x * _sigmoid(x)


def _dot(a, b):
    return jnp.dot(a.astype(BF16), b.astype(BF16), preferred_element_type=F32)


def _dot_nt(a, b):
    return lax.dot_general(a.astype(BF16), b.astype(BF16), (((1,), (1,)), ((), ())),
                           preferred_element_type=F32)


def _dot_tn(a, b):
    return lax.dot_general(a.astype(BF16), b.astype(BF16), (((0,), (0,)), ((), ())),
                           preferred_element_type=F32)


def _split(a):
    hi = a.astype(BF16)
    lo = (a - hi.astype(F32)).astype(BF16)
    return hi, lo


NORM_ROWS = 256


def _normalize_rows(x_ref, g_ref, h_ref):
    tm = x_ref.shape[0]
    step = min(tm, NORM_ROWS)
    for r in range(0, tm, step):
        x = x_ref[r:r + step, :]
        ms = jnp.mean(x * x, axis=-1, keepdims=True)
        h_ref[r:r + step, :] = (x * lax.rsqrt(ms + EPS) * g_ref[...]).astype(BF16)


def _rms_mm_body(x_ref, g_ref, w_ref, o_ref, h_ref):
    @pl.when(pl.program_id(1) == 0)
    def _():
        _normalize_rows(x_ref, g_ref, h_ref)

    o_ref[...] = jnp.dot(h_ref[...], w_ref[...], preferred_element_type=F32)


def rms_matmul(x, g, w_all, li, tm, tn):
    m, k = x.shape
    n = w_all.shape[2]
    return pl.pallas_call(
        _rms_mm_body,
        grid=(m // tm, n // tn),
        in_specs=[
            pl.BlockSpec((tm, k), lambda i, j: (i, 0)),
            pl.BlockSpec((1, k), lambda i, j: (0, 0)),
            pl.BlockSpec((None, k, tn), lambda i, j: (li, 0, j)),
        ],
        out_specs=pl.BlockSpec((tm, tn), lambda i, j: (i, j)),
        out_shape=jax.ShapeDtypeStruct((m, n), F32),
        scratch_shapes=[pltpu.VMEM((tm, k), BF16)],
        compiler_params=_cparams("parallel", "arbitrary"),
        name="rms_matmul",
    )(x, g.reshape(1, k), w_all)


def _rms_mm_glu_body(x_ref, g_ref, wv_ref, wg_ref, wz_ref, u_ref, sz_ref, h_ref):
    @pl.when(pl.program_id(1) == 0)
    def _():
        _normalize_rows(x_ref, g_ref, h_ref)

    h = h_ref[...]
    val = jnp.dot(h, wv_ref[...], preferred_element_type=F32)
    gate = jnp.dot(h, wg_ref[...], preferred_element_type=F32)
    z = jnp.dot(h, wz_ref[...], preferred_element_type=F32)
    u_ref[...] = val * _sigmoid(gate)
    sz_ref[...] = _silu(z)


def rms_matmul_glu(x, g, w_all, li, tm, tn):
    m, k = x.shape
    n = w_all.shape[2] // 3
    nt = n // tn
    return pl.pallas_call(
        _rms_mm_glu_body,
        grid=(m // tm, nt),
        in_specs=[
            pl.BlockSpec((tm, k), lambda i, j: (i, 0)),
            pl.BlockSpec((1, k), lambda i, j: (0, 0)),
            pl.BlockSpec((None, k, tn), lambda i, j: (li, 0, j)),
            pl.BlockSpec((None, k, tn), lambda i, j: (li, 0, j + nt)),
            pl.BlockSpec((None, k, tn), lambda i, j: (li, 0, j + 2 * nt)),
        ],
        out_specs=[
            pl.BlockSpec((tm, tn), lambda i, j: (i, j)),
            pl.BlockSpec((tm, tn), lambda i, j: (i, j)),
        ],
        out_shape=[jax.ShapeDtypeStruct((m, n), F32), jax.ShapeDtypeStruct((m, n), F32)],
        scratch_shapes=[pltpu.VMEM((tm, k), BF16)],
        compiler_params=_cparams("parallel", "arbitrary"),
        name="rms_matmul_glu",
    )(x, g.reshape(1, k), w_all, w_all, w_all)


def _mm_res_body(*refs, n_in, final):
    a_refs = refs[:n_in]
    w_refs = refs[n_in:2 * n_in]
    res_ref = refs[2 * n_in]
    o_ref = refs[-1]
    acc = res_ref[...]
    for a_ref, w_ref in zip(a_refs, w_refs):
        acc = acc + jnp.dot(a_ref[...].astype(BF16), w_ref[...], preferred_element_type=F32)
    if final:
        g_ref = refs[2 * n_in + 1]
        ms = jnp.mean(acc * acc, axis=-1, keepdims=True)
        acc = acc * lax.rsqrt(ms + EPS) * g_ref[...]
    o_ref[...] = acc


def matmul_residual(acts, w_all, li, res, tm, final_gain=None):
    m, n = res.shape
    n_in = len(acts)
    kb = acts[0].shape[1]
    assert all(a.shape[1] == kb for a in acts) and n_in * kb == w_all.shape[1]
    final = final_gain is not None
    in_specs = [pl.BlockSpec((tm, kb), lambda i: (i, 0)) for _ in acts]
    in_specs += [pl.BlockSpec((None, kb, n), lambda i, p=p: (li, p, 0)) for p in range(n_in)]
    in_specs += [pl.BlockSpec((tm, n), lambda i: (i, 0))]
    args = list(acts) + [w_all] * n_in + [res]
    if final:
        in_specs += [pl.BlockSpec((1, n), lambda i: (0, 0))]
        args += [final_gain.reshape(1, n)]
    return pl.pallas_call(
        functools.partial(_mm_res_body, n_in=n_in, final=final),
        grid=(m // tm,),
        in_specs=in_specs,
        out_specs=pl.BlockSpec((tm, n), lambda i: (i, 0)),
        out_shape=jax.ShapeDtypeStruct((m, n), F32),
        compiler_params=_cparams("parallel"),
        name="matmul_residual",
    )(*args)


def _attn_body(sink_ref, q_ref, za_ref, kp_ref, kc_ref, vp_ref, vc_ref, o_ref, *, tq, first_block_has_no_prev):
    q = q_ref[...]
    za = za_ref[...]

    def keys(prev_ref, cur_ref):
        cur = cur_ref[...]
        if tq < WINDOW:
            cur = jnp.concatenate([cur, jnp.zeros((WINDOW - tq, D_KV_A), F32)], axis=0)
        return jnp.concatenate([prev_ref[...], cur], axis=0).astype(BF16)

    k = keys(kp_ref, kc_ref)
    v = keys(vp_ref, vc_ref)
    rows = GROUP_A * tq
    r = lax.broadcasted_iota(jnp.int32, (rows, 2 * WINDOW), 0)
    c = lax.broadcasted_iota(jnp.int32, (rows, 2 * WINDOW), 1)
    i = r & (tq - 1)
    mask = (c >= i) & (c <= i + WINDOW)
    if first_block_has_no_prev:
        mask = mask & ((c >= WINDOW) | (pl.program_id(1) > 0))
    heads = range(N_KV_A)
    scores = []
    for h in heads:
        base = h * GROUP_A * HD_A
        qs = jnp.concatenate([q[:, base + g * HD_A: base + (g + 1) * HD_A] for g in range(GROUP_A)], axis=0)
        scores.append(_dot_nt(qs, k[:, h * HD_A:(h + 1) * HD_A]))
    probs = []
    for h in heads:
        s = jnp.where(mask, scores[h] * (HD_A ** -0.5), -jnp.inf)
        sk = jnp.concatenate([jnp.full((tq, 1), sink_ref[h * GROUP_A + g], F32) for g in range(GROUP_A)], axis=0)
        m = jnp.maximum(jnp.max(s, axis=-1, keepdims=True), sk)
        e = jnp.exp(s - m)
        probs.append(e / (jnp.sum(e, axis=-1, keepdims=True) + jnp.exp(sk - m)))
    outs = []
    for h in heads:
        o = _dot(probs[h], v[:, h * HD_A:(h + 1) * HD_A])
        outs += [o[g * tq:(g + 1) * tq] for g in range(GROUP_A)]
    o_all = jnp.concatenate(outs, axis=1)
    o_ref[...] = (o_all * _silu(za)).astype(o_ref.dtype)


def attention(u, sinks, prev_k, prev_v, prev_spec, nseq, t, tq, first_block_has_no_prev):
    nb = t // tq
    row = lambda n, j: n * nb + j
    return pl.pallas_call(
        functools.partial(_attn_body, tq=tq, first_block_has_no_prev=first_block_has_no_prev),
        grid=(nseq, nb),
        in_specs=[
            pl.BlockSpec(memory_space=pltpu.SMEM),
            pl.BlockSpec((tq, D_A), lambda n, j: (row(n, j), COL_QA // D_A)),
            pl.BlockSpec((tq, D_A), lambda n, j: (row(n, j), COL_ZA // D_A)),
            prev_spec(COL_KA // D_KV_A),
            pl.BlockSpec((tq, D_KV_A), lambda n, j: (row(n, j), COL_KA // D_KV_A)),
            prev_spec(COL_VA // D_KV_A),
            pl.BlockSpec((tq, D_KV_A), lambda n, j: (row(n, j), COL_VA // D_KV_A)),
        ],
        out_specs=pl.BlockSpec((tq, D_A), lambda n, j: (row(n, j), 0)),
        out_shape=jax.ShapeDtypeStruct((nseq * t, D_A), _act_dtype(tq)),
        compiler_params=_cparams("parallel", "arbitrary"),
        name="swa_attention",
    )(sinks, u, u, prev_k, u, prev_v, u)


DN_HALO = SUBLANES


def _softplus(x):
    return jnp.maximum(x, 0.0) + jnp.log1p(jnp.exp(-jnp.abs(x)))


def _deltanet_body(*refs, sb, tv, nc, has_state):
    if has_state:
        (q_ref, k_ref, v_ref, ab_ref, zb_ref, cprev_ref, s0_ref, cw_ref, alog_ref, dtb_ref, ng_ref,
         o_ref, sfin_ref, xp_ref, s_ref) = refs
    else:
        (q_ref, k_ref, v_ref, ab_ref, zb_ref, cw_ref, alog_ref, dtb_ref, ng_ref,
         o_ref, sfin_ref, xp_ref, s_ref) = refs
    c = pl.program_id(1)
    ch = DN_CHUNK
    halo = DN_HALO
    first_tap_row = halo - (DN_CONV - 1)

    @pl.when(c == 0)
    def _():
        if has_state:
            s_ref[...] = s0_ref[...]
            xp_ref[:, first_tap_row:halo, :] = cprev_ref[...]
        else:
            s_ref[...] = jnp.zeros(s_ref.shape, F32)
            xp_ref[:, 0:halo, :] = jnp.zeros((sb, halo, D_QKV_B), F32)

    ii = lax.broadcasted_iota(jnp.int32, (ch, ch), 0)
    jj = lax.broadcasted_iota(jnp.int32, (ch, ch), 1)
    lower = ii >= jj
    strict = ii > jj
    eye = (ii == jj).astype(F32)
    tril_ones = lower.astype(F32)
    sel_rows = (lax.broadcasted_iota(jnp.int32, (H_B, LANES), 0)
                == lax.broadcasted_iota(jnp.int32, (H_B, LANES), 1)).astype(F32)
    row_id = lax.broadcasted_iota(jnp.int32, (ch, 1), 0)
    cw = cw_ref[...]
    hp = lax.Precision.HIGHEST

    units = []
    for n in range(sb):
        xp_ref[n, halo:halo + tv, 0:1024] = q_ref[n]
        xp_ref[n, halo:halo + tv, 1024:2048] = k_ref[n]
        xp_ref[n, halo:halo + tv, 2048:3072] = v_ref[n]
        if tv < ch:
            xp_ref[n, halo + tv:halo + ch, :] = jnp.zeros((ch - tv, D_QKV_B), F32)
        y = jnp.zeros((ch, D_QKV_B), F32)
        for tap in range(DN_CONV):
            off = first_tap_row + tap
            y = y + cw[tap:tap + 1, :] * xp_ref[n, off:off + ch, :]
        y = _silu(y)
        if tv < ch:
            y = jnp.where(row_id < tv, y, 0.0)
        if nc > 1:
            xp_ref[n, 0:halo, :] = xp_ref[n, tv:tv + halo, :]
        ab = ab_ref[n]
        if tv < ch:
            ab = jnp.concatenate([ab, jnp.zeros((ch - tv, LANES), F32)], axis=0)
        gpre = -jnp.exp(alog_ref[...]) * _softplus(ab + dtb_ref[...])
        if tv < ch:
            gpre = jnp.where(row_id < tv, gpre, 0.0)
        beta_all = _sigmoid(ab)
        gcum = jnp.dot(tril_ones, gpre, precision=hp, preferred_element_type=F32)
        gcum_t = lax.dot_general(sel_rows, gcum, (((1,), (1,)), ((), ())), precision=hp,
                                 preferred_element_type=F32)
        for h in range(H_B):
            qh = y[:, h * DK_B:(h + 1) * DK_B]
            kh = y[:, 1024 + h * DK_B:1024 + (h + 1) * DK_B]
            vh = y[:, 2048 + h * DV_B:2048 + (h + 1) * DV_B]
            qh = qh * lax.rsqrt(jnp.sum(qh * qh, axis=-1, keepdims=True) + EPS) * (DK_B ** -0.5)
            kh = kh * lax.rsqrt(jnp.sum(kh * kh, axis=-1, keepdims=True) + EPS)
            gc = gcum[:, h:h + 1]
            gr = gcum_t[h:h + 1, :]
            glast = gcum[tv - 1:tv, h:h + 1]
            beta = beta_all[:, H_B + h:H_B + h + 1]
            egc = jnp.exp(gc)
            kbeta = kh * beta
            units.append(dict(
                n=n, h=h, qh=qh, kh=kh, kbeta=kbeta, glast=glast,
                decay=jnp.where(lower, jnp.exp(jnp.where(lower, gc - gr, 0.0)), 0.0),
                rhs=jnp.concatenate([vh * beta, kbeta * egc], axis=1),
                qg=qh * egc, kg=kh * jnp.exp(glast - gc)))

    for u in units:
        kq = _dot_nt(jnp.concatenate([u["kbeta"], u["qh"]], axis=0), u["kh"])
        u["w"] = jnp.concatenate([-jnp.where(strict, kq[:ch] * u["decay"], 0.0), eye], axis=1)
        u["attn"] = kq[ch:] * u["decay"]

    right = lax.broadcasted_iota(jnp.int32, (ch, 2 * ch), 1) >= ch
    for _ in range(max(1, (tv - 1).bit_length())):
        prods = [_dot(u["w"][:, :ch], u["w"]) for u in units]
        for u, m in zip(units, prods):
            u["w"] = m + jnp.where(right, u["w"], 0.0)

    for u in units:
        hi, lo = _split(u["rhs"])
        sol = jnp.dot(u["w"][:, ch:].astype(BF16), jnp.concatenate([hi, lo], axis=1), preferred_element_type=F32)
        u["u"] = sol[:, 0:DV_B] + sol[:, 2 * DV_B:3 * DV_B]
        u["wk"] = sol[:, DV_B:2 * DV_B] + sol[:, 3 * DV_B:4 * DV_B]

    for u in units:
        ws_qs = _dot(jnp.concatenate([u["wk"], u["qg"]], axis=0), s_ref[u["n"], u["h"]])
        u["v_new"] = u["u"] - ws_qs[:ch]
        u["o"] = ws_qs[ch:]

    for u in units:
        u["o"] = u["o"] + _dot(u["attn"], u["v_new"])
        s_ref[u["n"], u["h"]] = s_ref[u["n"], u["h"]] * jnp.exp(u["glast"]) + _dot_tn(u["kg"], u["v_new"])

    for n in range(sb):
        outs = []
        for u in units[n * H_B:(n + 1) * H_B]:
            o = u["o"]
            outs.append(o * lax.rsqrt(jnp.mean(o * o, axis=-1, keepdims=True) + EPS) * ng_ref[...])
        o_all = jnp.concatenate(outs, axis=1)[:tv]
        o_ref[n] = (o_all * _silu(zb_ref[n])).astype(o_ref.dtype)

    @pl.when(c == nc - 1)
    def _():
        sfin_ref[...] = s_ref[...]


def deltanet(u3, state, li, conv_w, a_log, dt_bias, norm_g, sb):
    nseq, t, _ = u3.shape
    tv = min(DN_CHUNK, t)
    nc = t // tv
    has_state = state is not None
    pad8 = lambda x: jnp.zeros((1, LANES), F32).at[0, :H_B].set(x)
    blk = lambda col: pl.BlockSpec((sb, tv, 1024), lambda s, c: (s, c, col))
    const = lambda shape: pl.BlockSpec(shape, lambda s, c: tuple(0 for _ in shape))
    in_specs = [
        blk(COL_QKV_B // 1024), blk(COL_QKV_B // 1024 + 1), blk(COL_QKV_B // 1024 + 2),
        pl.BlockSpec((sb, tv, LANES), lambda s, c: (s, c, COL_AB // LANES)),
        blk(COL_ZB // 1024),
    ]
    args = [u3] * 5
    if has_state:
        in_specs += [
            pl.BlockSpec((None, sb, DN_CONV - 1, D_QKV_B), lambda s, c: (li, s, 0, 0)),
            pl.BlockSpec((None, sb, H_B, DK_B, DV_B), lambda s, c: (li, s, 0, 0, 0)),
        ]
        args += list(state)
    in_specs += [const((DN_CONV, D_QKV_B)), const((1, LANES)), const((1, LANES)), const((1, DV_B))]
    args += [conv_w, pad8(a_log), pad8(dt_bias), norm_g.reshape(1, DV_B)]
    return pl.pallas_call(
        functools.partial(_deltanet_body, sb=sb, tv=tv, nc=nc, has_state=has_state),
        grid=(nseq // sb, nc),
        in_specs=in_specs,
        out_specs=[
            pl.BlockSpec((sb, tv, D_B), lambda s, c: (s, c, 0)),
            pl.BlockSpec((sb, H_B, DK_B, DV_B), lambda s, c: (s, 0, 0, 0)),
        ],
        out_shape=[jax.ShapeDtypeStruct((nseq, t, D_B), _act_dtype(tv)),
                   jax.ShapeDtypeStruct((nseq, H_B, DK_B, DV_B), F32)],
        scratch_shapes=[pltpu.VMEM((sb, DN_HALO + DN_CHUNK, D_QKV_B), F32),
                        pltpu.VMEM((sb, H_B, DK_B, DV_B), F32)],
        compiler_params=_cparams("parallel", "arbitrary"),
        name="gated_deltanet",
    )(*args)


CF_HALO = 32
CF_FIRST = CF_HALO - (CF_CONV - 1)
CF_GROUPS_AHEAD = (CF_HALO + SUBLANES) // SUBLANES


def _tree_sum(terms):
    while len(terms) > 1:
        terms = [terms[i] + terms[i + 1] for i in range(0, len(terms) - 1, 2)] + terms[len(terms) & ~1:]
    return terms[0]


def _group_rows(g):
    if isinstance(g, int):
        return pl.ds(g * SUBLANES, SUBLANES)
    return pl.ds(pl.multiple_of(g * SUBLANES, SUBLANES), SUBLANES)


def _cfconv_body(*refs, tt, has_prev):
    if has_prev:
        u_ref, sz_ref, prev_ref, cw_ref, cb_ref, lg_ref, lb_ref, o_ref, xp_ref, wb_ref, acc_ref = refs
    else:
        u_ref, sz_ref, cw_ref, cb_ref, lg_ref, lb_ref, o_ref, xp_ref, wb_ref, acc_ref = refs

    @pl.when(pl.program_id(1) == 0)
    def _():
        for tap in range(CF_CONV):
            wb_ref[tap] = jnp.broadcast_to(cw_ref[tap:tap + 1, :], (SUBLANES, D_C))
        xp_ref[0:CF_HALO, :] = jnp.zeros((CF_HALO, D_C), F32)
        if has_prev:
            xp_ref[CF_FIRST:CF_HALO, :] = prev_ref[...]
        xp_ref[CF_HALO + tt:CF_HALO + tt + SUBLANES, :] = jnp.zeros((SUBLANES, D_C), F32)

    xp_ref[CF_HALO:CF_HALO + tt, :] = u_ref[...]

    row = lax.broadcasted_iota(jnp.int32, (SUBLANES, LANES), 0)
    ngrp = tt // SUBLANES
    for ci in range(D_C // LANES):
        cols = slice(ci * LANES, (ci + 1) * LANES)

        def zgroup(g, cols=cols):
            xs = [xp_ref[_group_rows(g + a), cols] for a in range(CF_GROUPS_AHEAD)]
            zs = []
            for b in range(SUBLANES):
                terms = [wb_ref[SUBLANES * a + b - CF_FIRST, :, cols] * xs[a]
                         for a in range(CF_GROUPS_AHEAD) if 0 <= SUBLANES * a + b - CF_FIRST < CF_CONV]
                zs.append(_tree_sum(terms))
            return tuple(zs)

        def body(g, zprev, cols=cols, zgroup=zgroup):
            znext = zgroup(g + 1)
            terms = [zprev[0]] + [pltpu.roll(jnp.where(row < b, znext[b], zprev[b]), SUBLANES - b, axis=0)
                                  for b in range(1, SUBLANES)]
            acc_ref[_group_rows(g), cols] = _tree_sum(terms)
            return znext

        lax.fori_loop(0, ngrp, body, zgroup(0), unroll=4 if ngrp % 4 == 0 else 1)

    rows = 2 * SUBLANES if tt % (2 * SUBLANES) == 0 else SUBLANES

    def norm_rows(g, carry):
        sl = pl.ds(pl.multiple_of(g * rows, rows), rows)
        acc = acc_ref[sl, :] + cb_ref[...]
        xc = acc - jnp.mean(acc, axis=-1, keepdims=True)
        y = xc * lax.rsqrt(jnp.mean(xc * xc, axis=-1, keepdims=True) + EPS)
        y = y * lg_ref[...] + lb_ref[...]
        o_ref[sl, :] = (_silu(y) * sz_ref[sl, :]).astype(o_ref.dtype)
        return carry

    ngrp_norm = tt // rows
    lax.fori_loop(0, ngrp_norm, norm_rows, 0, unroll=4 if ngrp_norm % 4 == 0 else 1)
    if tt >= CF_HALO:
        xp_ref[0:CF_HALO, :] = xp_ref[tt:tt + CF_HALO, :]


def conformer_conv(u, sz, prev_all, li, conv_w, conv_b, ln_g, ln_b, nseq, t, tt):
    nb = t // tt
    assert nb == 1 or tt >= CF_HALO
    has_prev = prev_all is not None
    row = lambda n, j: (n * nb + j, 0)
    const = lambda shape: pl.BlockSpec(shape, lambda n, j: (0, 0))
    in_specs = [pl.BlockSpec((tt, D_C), row), pl.BlockSpec((tt, D_C), row)]
    args = [u, sz]
    if has_prev:
        in_specs += [pl.BlockSpec((None, None, CF_CONV - 1, D_C), lambda n, j: (li, n, 0, 0))]
        args += [prev_all]
    in_specs += [const((CF_CONV, D_C)), const((1, D_C)), const((1, D_C)), const((1, D_C))]
    args += [conv_w, conv_b.reshape(1, D_C), ln_g.reshape(1, D_C), ln_b.reshape(1, D_C)]
    return pl.pallas_call(
        functools.partial(_cfconv_body, tt=tt, has_prev=has_prev),
        grid=(nseq, nb),
        in_specs=in_specs,
        out_specs=pl.BlockSpec((tt, D_C), row),
        out_shape=jax.ShapeDtypeStruct((nseq * t, D_C), _act_dtype(tt)),
        scratch_shapes=[pltpu.VMEM((CF_HALO + tt + SUBLANES, D_C), F32),
                        pltpu.VMEM((CF_CONV, SUBLANES, D_C), F32),
                        pltpu.VMEM((tt, D_C), F32)],
        compiler_params=_cparams("parallel", "arbitrary"),
        name="conformer_conv",
    )(*args)


def _prep_even_weight(w):
    sl = lambda a, b: w[:, :, a:b]
    parts = [sl(0, 1024), sl(1536, 2560), sl(2560, 5632), sl(5648, 6672), sl(1024, 1280), sl(1280, 1536),
             sl(5632, 5648), jnp.zeros(w.shape[:2] + (N_EVEN_PAD - N_EVEN,), w.dtype)]
    return jnp.concatenate(parts, axis=2).astype(BF16)


def _trunk(x, caches, weights, tm_in, tm_out, tq, dn_sb, tt):
    (norm_gain, w_in_even, w_out_even, attn_sinks, dn_conv_w, dn_a_log, dn_dt_bias, dn_norm_gain,
     w_in_odd, w_out_odd, cf_conv_w, cf_conv_b, cf_ln_gain, cf_ln_bias, final_norm_gain) = weights
    nseq, t, _ = x.shape
    m = nseq * t
    x = x.reshape(m, D_MODEL)
    if caches is not None:
        cache_k, cache_v, state_dn, state_dn_conv, state_cf_conv = caches
        n_even = cache_k.shape[0]
        ck = cache_k.reshape(n_even, nseq * WINDOW, D_KV_A)
        cv = cache_v.reshape(n_even, nseq * WINDOW, D_KV_A)
    new_k, new_v, new_s, new_qkv, new_u = [], [], [], [], []
    for layer in range(DEPTH):
        i = layer // 2
        last = layer == DEPTH - 1
        if layer % 2 == 0:
            u = rms_matmul(x, norm_gain[layer], w_in_even, i, tm_in, TN_EVEN)
            u3 = u.reshape(nseq, t, N_EVEN_PAD)
            if caches is None:
                nb = t // tq
                prev_spec = lambda col: pl.BlockSpec(
                    (WINDOW, D_KV_A), lambda n, j, col=col: (n * nb + jnp.maximum(j - 1, 0), col))
                o_a = attention(u, attn_sinks[i], u, u, prev_spec, nseq, t, tq, True)
                dn_state = None
            else:
                prev_spec = lambda col, i=i: pl.BlockSpec((None, WINDOW, D_KV_A), lambda n, j: (i, n, 0))
                o_a = attention(u, attn_sinks[i], ck, cv, prev_spec, nseq, t, tq, False)
                dn_state = (state_dn_conv, state_dn)
            o_b, s_new = deltanet(u3, dn_state, i, dn_conv_w[i], dn_a_log[i], dn_dt_bias[i], dn_norm_gain[i], dn_sb)
            x = matmul_residual([o_a, o_b.reshape(m, D_B)], w_out_even, i, x, tm_out)
            keep = min(t, WINDOW)
            new_k.append(u3[:, t - keep:, COL_KA:COL_KA + D_KV_A])
            new_v.append(u3[:, t - keep:, COL_VA:COL_VA + D_KV_A])
            new_s.append(s_new)
            new_qkv.append(u3[:, t - min(t, DN_CONV - 1):, COL_QKV_B:COL_QKV_B + D_QKV_B])
        else:
            uu, sz = rms_matmul_glu(x, norm_gain[layer], w_in_odd, i, tm_in, TN_ODD)
            prev_all = None if caches is None else state_cf_conv
            y = conformer_conv(uu, sz, prev_all, i, cf_conv_w[i], cf_conv_b[i], cf_ln_gain[i], cf_ln_bias[i],
                               nseq, t, tt)
            x = matmul_residual([y], w_out_odd, i, x, tm_out, final_norm_gain if last else None)
            new_u.append(uu.reshape(nseq, t, D_C)[:, t - min(t, CF_CONV - 1):])

    def with_history(old, new, length):
        new = jnp.stack(new)
        if new.shape[2] >= length:
            return new[:, :, new.shape[2] - length:]
        if old is None:
            old = jnp.zeros(new.shape[:2] + (length,) + new.shape[3:], new.dtype)
        return jnp.concatenate([old[:, :, new.shape[2]:].reshape(new.shape[:2] + (-1,) + new.shape[3:]), new], axis=2)

    old_k = old_v = old_dnc = old_cfc = None
    if caches is not None:
        old_k = cache_k.reshape(cache_k.shape[:3] + (D_KV_A,))
        old_v = cache_v.reshape(cache_v.shape[:3] + (D_KV_A,))
        old_dnc, old_cfc = state_dn_conv, state_cf_conv
    heads = lambda z: z.reshape(z.shape[:3] + (N_KV_A, HD_A))
    states = (heads(with_history(old_k, new_k, WINDOW)), heads(with_history(old_v, new_v, WINDOW)),
              jnp.stack(new_s), with_history(old_dnc, new_qkv, DN_CONV - 1),
              with_history(old_cfc, new_u, CF_CONV - 1))
    return x.reshape(nseq, t, D_MODEL), states


def kernel(x_prompt, x_sample, cache_win_k, cache_win_v, state_dn, state_dn_conv, state_cf_conv, norm_gain, w_in_even, w_out_even, attn_sinks, dn_conv_w, dn_a_log, dn_dt_bias, dn_norm_gain, w_in_odd, w_out_odd, cf_conv_w, cf_conv_b, cf_ln_gain, cf_ln_bias, final_norm_gain):
    weights = (norm_gain, _prep_even_weight(w_in_even), w_out_even.astype(BF16), attn_sinks, dn_conv_w, dn_a_log,
               dn_dt_bias, dn_norm_gain, w_in_odd.astype(BF16), w_out_odd.astype(BF16), cf_conv_w, cf_conv_b,
               cf_ln_gain, cf_ln_bias, final_norm_gain)
    y_prompt, (p_k, p_v, p_dn, p_dnc, p_cfc) = _trunk(
        x_prompt, None, weights, tm_in=1024, tm_out=512, tq=WINDOW, dn_sb=2, tt=512)
    caches = (cache_win_k, cache_win_v, state_dn, state_dn_conv, state_cf_conv)
    y_sample, (s_k, s_v, s_dn, s_dnc, s_cfc) = _trunk(
        x_sample, caches, weights, tm_in=256, tm_out=256, tq=x_sample.shape[1], dn_sb=4, tt=x_sample.shape[1])
    return (y_prompt, y_sample, p_k, p_v, p_dn, p_dnc, p_cfc, s_k, s_v, s_dn, s_dnc, s_cfc)
```

```python
import functools

import jax
import jax.numpy as jnp
from jax import lax
from jax.experimental import pallas as pl
from jax.experimental.pallas import tpu as pltpu

F32 = jnp.float32
BF16 = jnp.bfloat16

D_MODEL = 2048
DEPTH = 4
HD_A = 64
N_KV_A = 4
GROUP_A = 4
WINDOW = 128
D_A = 1024
D_KV_A = N_KV_A * HD_A
DK_B = 128
DV_B = 128
H_B = 8
D_B = 1024
D_QKV_B = 3072
DN_CONV = 4
DN_CHUNK = 64
D_C = 2048
CF_CONV = 31
EPS = 1e-6

COL_QA = 0
COL_KA = 1024
COL_VA = 1280
COL_ZA = 1536
COL_QKV_B = 2560
N_EVEN_MAIN = 5632
COL_ZB = N_EVEN_MAIN
COL_AB = COL_ZB + 1024
N_EVEN = 6672
TN_EVEN = 512
N_EVEN_TAIL = 3 * TN_EVEN
N_EVEN_PAD = N_EVEN_MAIN + N_EVEN_TAIL
HALF = 512
TN_ODD = 512

LANES = 128
SUBLANES = 8
VMEM_LIMIT = 56 * 1024 * 1024


def _cparams(*sem):
    return pltpu.CompilerParams(dimension_semantics=sem, vmem_limit_bytes=VMEM_LIMIT)


def _act_dtype(block_rows):
    return BF16 if block_rows % 16 == 0 else F32


def _sigmoid(x):
    return 1.0 / (1.0 + jnp.exp(-x))


def _silu(x):
    return x * _sigmoid(x)


def _dot(a, b):
    return jnp.dot(a.astype(BF16), b.astype(BF16), preferred_element_type=F32)


def _dot_nt(a, b):
    return lax.dot_general(a.astype(BF16), b.astype(BF16), (((1,), (1,)), ((), ())),
                           preferred_element_type=F32)


def _dot_tn(a, b):
    return lax.dot_general(a.astype(BF16), b.astype(BF16), (((0,), (0,)), ((), ())),
                           preferred_element_type=F32)


def _split(a):
    hi = a.astype(BF16)
    lo = (a - hi.astype(F32)).astype(BF16)
    return hi, lo


NORM_ROWS = 256


def _normalize_rows(x_ref, g_ref, h_ref):
    tm = x_ref.shape[0]
    step = min(tm, NORM_ROWS)
    for r in range(0, tm, step):
        x = x_ref[r:r + step, :]
        ms = jnp.mean(x * x, axis=-1, keepdims=True)
        h_ref[r:r + step, :] = (x * lax.rsqrt(ms + EPS) * g_ref[...]).astype(BF16)


def _rms_mm_body(x_ref, g_ref, wa_ref, wb_ref, o_ref, h_ref, *, na):
    j = pl.program_id(1)

    @pl.when(j == 0)
    def _():
        _normalize_rows(x_ref, g_ref, h_ref)

    @pl.when(j < na)
    def _():
        o_ref[...] = jnp.dot(h_ref[...], wa_ref[...], preferred_element_type=F32)

    @pl.when(j >= na)
    def _():
        o_ref[...] = jnp.dot(h_ref[...], wb_ref[...], preferred_element_type=F32)


def rms_matmul(x, g, wa_all, wb_all, li, tm, tn):
    m, k = x.shape
    na = wa_all.shape[2] // tn
    nb = wb_all.shape[2] // tn
    return pl.pallas_call(
        functools.partial(_rms_mm_body, na=na),
        grid=(m // tm, na + nb),
        in_specs=[
            pl.BlockSpec((tm, k), lambda i, j: (i, 0)),
            pl.BlockSpec((1, k), lambda i, j: (0, 0)),
            pl.BlockSpec((None, k, tn), lambda i, j: (li, 0, jnp.minimum(j, na - 1))),
            pl.BlockSpec((None, k, tn), lambda i, j: (li, 0, jnp.maximum(j - na, 0))),
        ],
        out_specs=pl.BlockSpec((tm, tn), lambda i, j: (i, j)),
        out_shape=jax.ShapeDtypeStruct((m, (na + nb) * tn), F32),
        scratch_shapes=[pltpu.VMEM((tm, k), BF16)],
        compiler_params=_cparams("parallel", "arbitrary"),
        name="rms_matmul",
    )(x, g.reshape(1, k), wa_all, wb_all)


def _rms_mm_glu_body(x_ref, g_ref, wv_ref, wg_ref, wz_ref, u_ref, sz_ref, h_ref):
    @pl.when(pl.program_id(1) == 0)
    def _():
        _normalize_rows(x_ref, g_ref, h_ref)

    h = h_ref[...]
    val = jnp.dot(h, wv_ref[...], preferred_element_type=F32)
    gate = jnp.dot(h, wg_ref[...], preferred_element_type=F32)
    z = jnp.dot(h, wz_ref[...], preferred_element_type=F32)
    u_ref[...] = val * _sigmoid(gate)
    sz_ref[...] = _silu(z)


def rms_matmul_glu(x, g, w_all, li, tm, tn):
    m, k = x.shape
    n = w_all.shape[2] // 3
    nt = n // tn
    return pl.pallas_call(
        _rms_mm_glu_body,
        grid=(m // tm, nt),
        in_specs=[
            pl.BlockSpec((tm, k), lambda i, j: (i, 0)),
            pl.BlockSpec((1, k), lambda i, j: (0, 0)),
            pl.BlockSpec((None, k, tn), lambda i, j: (li, 0, j)),
            pl.BlockSpec((None, k, tn), lambda i, j: (li, 0, j + nt)),
            pl.BlockSpec((None, k, tn), lambda i, j: (li, 0, j + 2 * nt)),
        ],
        out_specs=[
            pl.BlockSpec((tm, tn), lambda i, j: (i, j)),
            pl.BlockSpec((tm, tn), lambda i, j: (i, j)),
        ],
        out_shape=[jax.ShapeDtypeStruct((m, n), F32), jax.ShapeDtypeStruct((m, n), F32)],
        scratch_shapes=[pltpu.VMEM((tm, k), BF16)],
        compiler_params=_cparams("parallel", "arbitrary"),
        name="rms_matmul_glu",
    )(x, g.reshape(1, k), w_all, w_all, w_all)


def _mm_res_body(*refs, n_in, final):
    a_refs = refs[:n_in]
    w_refs = refs[n_in:2 * n_in]
    res_ref = refs[2 * n_in]
    o_ref = refs[-1]
    acc = res_ref[...]
    for a_ref, w_ref in zip(a_refs, w_refs):
        acc = acc + jnp.dot(a_ref[...].astype(BF16), w_ref[...], preferred_element_type=F32)
    if final:
        g_ref = refs[2 * n_in + 1]
        ms = jnp.mean(acc * acc, axis=-1, keepdims=True)
        acc = acc * lax.rsqrt(ms + EPS) * g_ref[...]
    o_ref[...] = acc


def matmul_residual(acts, w_all, li, res, tm, final_gain=None):
    m, n = res.shape
    n_in = len(acts)
    kb = acts[0].shape[1]
    assert all(a.shape[1] == kb for a in acts) and n_in * kb == w_all.shape[1]
    final = final_gain is not None
    in_specs = [pl.BlockSpec((tm, kb), lambda i: (i, 0)) for _ in acts]
    in_specs += [pl.BlockSpec((None, kb, n), lambda i, p=p: (li, p, 0)) for p in range(n_in)]
    in_specs += [pl.BlockSpec((tm, n), lambda i: (i, 0))]
    args = list(acts) + [w_all] * n_in + [res]
    if final:
        in_specs += [pl.BlockSpec((1, n), lambda i: (0, 0))]
        args += [final_gain.reshape(1, n)]
    return pl.pallas_call(
        functools.partial(_mm_res_body, n_in=n_in, final=final),
        grid=(m // tm,),
        in_specs=in_specs,
        out_specs=pl.BlockSpec((tm, n), lambda i: (i, 0)),
        out_shape=jax.ShapeDtypeStruct((m, n), F32),
        compiler_params=_cparams("parallel"),
        name="matmul_residual",
    )(*args)


def _attn_body(sink_ref, q_ref, za0_ref, za1_ref, kp_ref, kc_ref, vp_ref, vc_ref, o_ref, *, tq, nsub,
               first_block_has_no_prev):
    q = q_ref[...]
    za = jnp.concatenate([za0_ref[...], za1_ref[...]], axis=1)

    def keys(prev_ref, cur_ref):
        cur = cur_ref[...]
        if tq < WINDOW:
            cur = jnp.concatenate([cur, jnp.zeros((WINDOW - tq, D_KV_A), F32)], axis=0)
        return jnp.concatenate([prev_ref[...], cur], axis=0)

    k = keys(kp_ref, kc_ref)
    v = keys(vp_ref, vc_ref)
    rows = GROUP_A * tq
    r = lax.broadcasted_iota(jnp.int32, (rows, 2 * WINDOW), 0)
    c = lax.broadcasted_iota(jnp.int32, (rows, 2 * WINDOW), 1)
    i = r & (tq - 1)
    mask = (c >= i) & (c <= i + WINDOW)
    masks = [mask] * nsub
    if first_block_has_no_prev:
        masks[0] = mask & ((c >= WINDOW) | (pl.program_id(1) > 0))
    ones_col = (lax.broadcasted_iota(jnp.int32, (2 * WINDOW, HD_A), 1) == 0).astype(F32)
    units = [(s, h) for s in range(nsub) for h in range(N_KV_A)]
    scores = []
    for s, h in units:
        base = h * GROUP_A * HD_A
        qs = jnp.concatenate([q[s * tq:(s + 1) * tq, base + g * HD_A: base + (g + 1) * HD_A]
                              for g in range(GROUP_A)], axis=0)
        scores.append(_dot_nt(qs, k[s * WINDOW:(s + 2) * WINDOW, h * HD_A:(h + 1) * HD_A]))
    exps, sink_terms = [], []
    for (s, h), sc in zip(units, scores):
        sc = jnp.where(masks[s], sc * (HD_A ** -0.5), -jnp.inf)
        sk = jnp.concatenate([jnp.full((tq, 1), sink_ref[h * GROUP_A + g], F32) for g in range(GROUP_A)], axis=0)
        m = jnp.maximum(jnp.max(sc, axis=-1, keepdims=True), sk)
        exps.append(jnp.exp(sc - m).astype(BF16))
        sink_terms.append(jnp.exp(sk - m))
    outs = [[] for _ in range(nsub)]
    for (s, h), e, st in zip(units, exps, sink_terms):
        v_ext = jnp.concatenate([v[s * WINDOW:(s + 2) * WINDOW, h * HD_A:(h + 1) * HD_A], ones_col], axis=1)
        pv = jnp.dot(e, v_ext.astype(BF16), preferred_element_type=F32)
        o = pv[:, :HD_A] / (pv[:, HD_A:HD_A + 1] + st)
        outs[s] += [o[g * tq:(g + 1) * tq] for g in range(GROUP_A)]
    o_all = jnp.concatenate([jnp.concatenate(o, axis=1) for o in outs], axis=0)
    o_ref[...] = (o_all * _silu(za)).astype(o_ref.dtype)


def attention(u, sinks, prev_k, prev_v, prev_spec, nseq, t, tq, nsub, first_block_has_no_prev):
    bq = tq * nsub
    nb = t // bq
    row = lambda n, j: n * nb + j
    return pl.pallas_call(
        functools.partial(_attn_body, tq=tq, nsub=nsub, first_block_has_no_prev=first_block_has_no_prev),
        grid=(nseq, nb),
        in_specs=[
            pl.BlockSpec(memory_space=pltpu.SMEM),
            pl.BlockSpec((bq, D_A), lambda n, j: (row(n, j), COL_QA // D_A)),
            pl.BlockSpec((bq, HALF), lambda n, j: (row(n, j), COL_ZA // HALF)),
            pl.BlockSpec((bq, HALF), lambda n, j: (row(n, j), COL_ZA // HALF + 1)),
            prev_spec(COL_KA // D_KV_A),
            pl.BlockSpec((bq, D_KV_A), lambda n, j: (row(n, j), COL_KA // D_KV_A)),
            prev_spec(COL_VA // D_KV_A),
            pl.BlockSpec((bq, D_KV_A), lambda n, j: (row(n, j), COL_VA // D_KV_A)),
        ],
        out_specs=pl.BlockSpec((bq, D_A), lambda n, j: (row(n, j), 0)),
        out_shape=jax.ShapeDtypeStruct((nseq * t, D_A), _act_dtype(bq)),
        compiler_params=_cparams("parallel", "arbitrary"),
        name="swa_attention",
    )(sinks, u, u, u, prev_k, u, prev_v, u)


DN_HALO = SUBLANES


def _softplus(x):
    return jnp.maximum(x, 0.0) + jnp.log1p(jnp.exp(-jnp.abs(x)))


def _deltanet_body(*refs, sb, tv, nc, has_state):
    n_qkv = D_QKV_B // HALF
    qkv_refs = refs[:n_qkv]
    ab_ref, zb0_ref, zb1_ref = refs[n_qkv:n_qkv + 3]
    if has_state:
        cprev_ref, s0_ref = refs[n_qkv + 3:n_qkv + 5]
    cw_ref, alog_ref, dtb_ref, ng_ref, o_ref, sfin_ref, xp_ref, s_ref = refs[-8:]
    c = pl.program_id(1)
    ch = DN_CHUNK
    halo = DN_HALO
    first_tap_row = halo - (DN_CONV - 1)

    @pl.when(c == 0)
    def _():
        if has_state:
            s_ref[...] = s0_ref[...]
            xp_ref[:, first_tap_row:halo, :] = cprev_ref[...]
        else:
            s_ref[...] = jnp.zeros(s_ref.shape, F32)
            xp_ref[:, 0:halo, :] = jnp.zeros((sb, halo, D_QKV_B), F32)

    ii = lax.broadcasted_iota(jnp.int32, (ch, ch), 0)
    jj = lax.broadcasted_iota(jnp.int32, (ch, ch), 1)
    lower = ii >= jj
    strict = ii > jj
    eye = (ii == jj).astype(F32)
    tril_ones = lower.astype(F32)
    sel_rows = (lax.broadcasted_iota(jnp.int32, (H_B, LANES), 0)
                == lax.broadcasted_iota(jnp.int32, (H_B, LANES), 1)).astype(F32)
    row_id = lax.broadcasted_iota(jnp.int32, (ch, 1), 0)
    cw = cw_ref[...]
    hp = lax.Precision.HIGHEST

    units = []
    for n in range(sb):
        for p, part_ref in enumerate(qkv_refs):
            xp_ref[n, halo:halo + tv, p * HALF:(p + 1) * HALF] = part_ref[n]
        if tv < ch:
            xp_ref[n, halo + tv:halo + ch, :] = jnp.zeros((ch - tv, D_QKV_B), F32)
        y = jnp.zeros((ch, D_QKV_B), F32)
        for tap in range(DN_CONV):
            off = first_tap_row + tap
            y = y + cw[tap:tap + 1, :] * xp_ref[n, off:off + ch, :]
        y = _silu(y)
        if tv < ch:
            y = jnp.where(row_id < tv, y, 0.0)
        if nc > 1:
            xp_ref[n, 0:halo, :] = xp_ref[n, tv:tv + halo, :]
        ab = ab_ref[n]
        if tv < ch:
            ab = jnp.concatenate([ab, jnp.zeros((ch - tv, LANES), F32)], axis=0)
        gpre = -jnp.exp(alog_ref[...]) * _softplus(ab + dtb_ref[...])
        if tv < ch:
            gpre = jnp.where(row_id < tv, gpre, 0.0)
        beta_all = _sigmoid(ab)
        gcum = jnp.dot(tril_ones, gpre, precision=hp, preferred_element_type=F32)
        gcum_t = lax.dot_general(sel_rows, gcum, (((1,), (1,)), ((), ())), precision=hp,
                                 preferred_element_type=F32)
        for h in range(H_B):
            qh = y[:, h * DK_B:(h + 1) * DK_B]
            kh = y[:, 1024 + h * DK_B:1024 + (h + 1) * DK_B]
            vh = y[:, 2048 + h * DV_B:2048 + (h + 1) * DV_B]
            qh = qh * lax.rsqrt(jnp.sum(qh * qh, axis=-1, keepdims=True) + EPS) * (DK_B ** -0.5)
            kh = kh * lax.rsqrt(jnp.sum(kh * kh, axis=-1, keepdims=True) + EPS)
            gc = gcum[:, h:h + 1]
            gr = gcum_t[h:h + 1, :]
            glast = gcum[tv - 1:tv, h:h + 1]
            beta = beta_all[:, H_B + h:H_B + h + 1]
            egc = jnp.exp(gc)
            kbeta = kh * beta
            units.append(dict(
                n=n, h=h, qh=qh, kh=kh, kbeta=kbeta, glast=glast,
                decay=jnp.where(lower, jnp.exp(jnp.where(lower, gc - gr, 0.0)), 0.0),
                rhs=jnp.concatenate([vh * beta, kbeta * egc], axis=1),
                qg=qh * egc, kg=kh * jnp.exp(glast - gc)))

    for u in units:
        kq = _dot_nt(jnp.concatenate([u["kbeta"], u["qh"]], axis=0), u["kh"])
        u["w"] = jnp.concatenate([-jnp.where(strict, kq[:ch] * u["decay"], 0.0), eye], axis=1)
        u["attn"] = kq[ch:] * u["decay"]

    right = lax.broadcasted_iota(jnp.int32, (ch, 2 * ch), 1) >= ch
    for _ in range(max(1, (tv - 1).bit_length())):
        prods = [_dot(u["w"][:, :ch], u["w"]) for u in units]
        for u, m in zip(units, prods):
            u["w"] = m + jnp.where(right, u["w"], 0.0)

    for u in units:
        hi, lo = _split(u["rhs"])
        sol = jnp.dot(u["w"][:, ch:].astype(BF16), jnp.concatenate([hi, lo], axis=1), preferred_element_type=F32)
        u["u"] = sol[:, 0:DV_B] + sol[:, 2 * DV_B:3 * DV_B]
        u["wk"] = sol[:, DV_B:2 * DV_B] + sol[:, 3 * DV_B:4 * DV_B]

    for u in units:
        ws_qs = _dot(jnp.concatenate([u["wk"], u["qg"]], axis=0), s_ref[u["n"], u["h"]])
        u["v_new"] = u["u"] - ws_qs[:ch]
        u["o"] = ws_qs[ch:]

    for u in units:
        u["o"] = u["o"] + _dot(u["attn"], u["v_new"])
        s_ref[u["n"], u["h"]] = s_ref[u["n"], u["h"]] * jnp.exp(u["glast"]) + _dot_tn(u["kg"], u["v_new"])

    for n in range(sb):
        outs = []
        for u in units[n * H_B:(n + 1) * H_B]:
            o = u["o"]
            outs.append(o * lax.rsqrt(jnp.mean(o * o, axis=-1, keepdims=True) + EPS) * ng_ref[...])
        o_all = jnp.concatenate(outs, axis=1)[:tv]
        zb = jnp.concatenate([zb0_ref[n], zb1_ref[n]], axis=1)
        o_ref[n] = (o_all * _silu(zb)).astype(o_ref.dtype)

    @pl.when(c == nc - 1)
    def _():
        sfin_ref[...] = s_ref[...]


def deltanet(u3, state, li, conv_w, a_log, dt_bias, norm_g, sb):
    nseq, t, _ = u3.shape
    tv = min(DN_CHUNK, t)
    nc = t // tv
    has_state = state is not None
    pad8 = lambda x: jnp.zeros((1, LANES), F32).at[0, :H_B].set(x)
    blk = lambda col: pl.BlockSpec((sb, tv, HALF), lambda s, c: (s, c, col))
    const = lambda shape: pl.BlockSpec(shape, lambda s, c: tuple(0 for _ in shape))
    in_specs = [blk(COL_QKV_B // HALF + p) for p in range(D_QKV_B // HALF)]
    in_specs += [pl.BlockSpec((sb, tv, LANES), lambda s, c: (s, c, COL_AB // LANES)),
                 blk(COL_ZB // HALF), blk(COL_ZB // HALF + 1)]
    args = [u3] * len(in_specs)
    if has_state:
        in_specs += [
            pl.BlockSpec((None, sb, DN_CONV - 1, D_QKV_B), lambda s, c: (li, s, 0, 0)),
            pl.BlockSpec((None, sb, H_B, DK_B, DV_B), lambda s, c: (li, s, 0, 0, 0)),
        ]
        args += list(state)
    in_specs += [const((DN_CONV, D_QKV_B)), const((1, LANES)), const((1, LANES)), const((1, DV_B))]
    args += [conv_w, pad8(a_log), pad8(dt_bias), norm_g.reshape(1, DV_B)]
    return pl.pallas_call(
        functools.partial(_deltanet_body, sb=sb, tv=tv, nc=nc, has_state=has_state),
        grid=(nseq // sb, nc),
        in_specs=in_specs,
        out_specs=[
            pl.BlockSpec((sb, tv, D_B), lambda s, c: (s, c, 0)),
            pl.BlockSpec((sb, H_B, DK_B, DV_B), lambda s, c: (s, 0, 0, 0)),
        ],
        out_shape=[jax.ShapeDtypeStruct((nseq, t, D_B), _act_dtype(tv)),
                   jax.ShapeDtypeStruct((nseq, H_B, DK_B, DV_B), F32)],
        scratch_shapes=[pltpu.VMEM((sb, DN_HALO + DN_CHUNK, D_QKV_B), F32),
                        pltpu.VMEM((sb, H_B, DK_B, DV_B), F32)],
        compiler_params=_cparams("parallel", "arbitrary"),
        name="gated_deltanet",
    )(*args)


CF_HALO = 32
CF_FIRST = CF_HALO - (CF_CONV - 1)
CF_GROUPS_AHEAD = (CF_HALO + SUBLANES) // SUBLANES


def _tree_sum(terms):
    while len(terms) > 1:
        terms = [terms[i] + terms[i + 1] for i in range(0, len(terms) - 1, 2)] + terms[len(terms) & ~1:]
    return terms[0]


def _group_rows(g):
    if isinstance(g, int):
        return pl.ds(g * SUBLANES, SUBLANES)
    return pl.ds(pl.multiple_of(g * SUBLANES, SUBLANES), SUBLANES)


def _cfconv_body(*refs, tt, has_prev):
    if has_prev:
        u_ref, sz_ref, prev_ref, cw_ref, cb_ref, lg_ref, lb_ref, o_ref, xp_ref, wb_ref, acc_ref = refs
    else:
        u_ref, sz_ref, cw_ref, cb_ref, lg_ref, lb_ref, o_ref, xp_ref, wb_ref, acc_ref = refs

    @pl.when(pl.program_id(1) == 0)
    def _():
        for tap in range(CF_CONV):
            wb_ref[tap] = jnp.broadcast_to(cw_ref[tap:tap + 1, :], (SUBLANES, D_C))
        xp_ref[0:CF_HALO, :] = jnp.zeros((CF_HALO, D_C), F32)
        if has_prev:
            xp_ref[CF_FIRST:CF_HALO, :] = prev_ref[...]
        xp_ref[CF_HALO + tt:CF_HALO + tt + SUBLANES, :] = jnp.zeros((SUBLANES, D_C), F32)

    xp_ref[CF_HALO:CF_HALO + tt, :] = u_ref[...]

    row = lax.broadcasted_iota(jnp.int32, (SUBLANES, LANES), 0)
    ngrp = tt // SUBLANES
    for ci in range(D_C // LANES):
        cols = slice(ci * LANES, (ci + 1) * LANES)

        def zgroup(g, cols=cols):
            xs = [xp_ref[_group_rows(g + a), cols] for a in range(CF_GROUPS_AHEAD)]
            zs = []
            for b in range(SUBLANES):
                terms = [wb_ref[SUBLANES * a + b - CF_FIRST, :, cols] * xs[a]
                         for a in range(CF_GROUPS_AHEAD) if 0 <= SUBLANES * a + b - CF_FIRST < CF_CONV]
                zs.append(_tree_sum(terms))
            return tuple(zs)

        def body(g, zprev, cols=cols, zgroup=zgroup):
            znext = zgroup(g + 1)
            terms = [zprev[0]] + [pltpu.roll(jnp.where(row < b, znext[b], zprev[b]), SUBLANES - b, axis=0)
                                  for b in range(1, SUBLANES)]
            acc_ref[_group_rows(g), cols] = _tree_sum(terms)
            return znext

        lax.fori_loop(0, ngrp, body, zgroup(0), unroll=4 if ngrp % 4 == 0 else 1)

    rows = 2 * SUBLANES if tt % (2 * SUBLANES) == 0 else SUBLANES

    def norm_rows(g, carry):
        sl = pl.ds(pl.multiple_of(g * rows, rows), rows)
        acc = acc_ref[sl, :] + cb_ref[...]
        xc = acc - jnp.mean(acc, axis=-1, keepdims=True)
        y = xc * lax.rsqrt(jnp.mean(xc * xc, axis=-1, keepdims=True) + EPS)
        y = y * lg_ref[...] + lb_ref[...]
        o_ref[sl, :] = (_silu(y) * sz_ref[sl, :]).astype(o_ref.dtype)
        return carry

    ngrp_norm = tt // rows
    lax.fori_loop(0, ngrp_norm, norm_rows, 0, unroll=4 if ngrp_norm % 4 == 0 else 1)
    if tt >= CF_HALO:
        xp_ref[0:CF_HALO, :] = xp_ref[tt:tt + CF_HALO, :]


def conformer_conv(u, sz, prev_all, li, conv_w, conv_b, ln_g, ln_b, nseq, t, tt):
    nb = t // tt
    assert nb == 1 or tt >= CF_HALO
    has_prev = prev_all is not None
    row = lambda n, j: (n * nb + j, 0)
    const = lambda shape: pl.BlockSpec(shape, lambda n, j: (0, 0))
    in_specs = [pl.BlockSpec((tt, D_C), row), pl.BlockSpec((tt, D_C), row)]
    args = [u, sz]
    if has_prev:
        in_specs += [pl.BlockSpec((None, None, CF_CONV - 1, D_C), lambda n, j: (li, n, 0, 0))]
        args += [prev_all]
    in_specs += [const((CF_CONV, D_C)), const((1, D_C)), const((1, D_C)), const((1, D_C))]
    args += [conv_w, conv_b.reshape(1, D_C), ln_g.reshape(1, D_C), ln_b.reshape(1, D_C)]
    return pl.pallas_call(
        functools.partial(_cfconv_body, tt=tt, has_prev=has_prev),
        grid=(nseq, nb),
        in_specs=in_specs,
        out_specs=pl.BlockSpec((tt, D_C), row),
        out_shape=jax.ShapeDtypeStruct((nseq * t, D_C), _act_dtype(tt)),
        scratch_shapes=[pltpu.VMEM((CF_HALO + tt + SUBLANES, D_C), F32),
                        pltpu.VMEM((CF_CONV, SUBLANES, D_C), F32),
                        pltpu.VMEM((tt, D_C), F32)],
        compiler_params=_cparams("parallel", "arbitrary"),
        name="conformer_conv",
    )(*args)


def _prep_even_weight(w):
    ab_end = N_EVEN_MAIN + 2 * H_B
    tail = [w[:, :, ab_end:N_EVEN], w[:, :, N_EVEN_MAIN:ab_end],
            jnp.zeros(w.shape[:2] + (N_EVEN_TAIL - (N_EVEN - N_EVEN_MAIN),), w.dtype)]
    return w[:, :, :N_EVEN_MAIN].astype(BF16), jnp.concatenate(tail, axis=2).astype(BF16)


def _trunk(x, caches, weights, tm_in, tm_out, tq, nsub, dn_sb, tt):
    (norm_gain, w_in_even, w_out_even, attn_sinks, dn_conv_w, dn_a_log, dn_dt_bias, dn_norm_gain,
     w_in_odd, w_out_odd, cf_conv_w, cf_conv_b, cf_ln_gain, cf_ln_bias, final_norm_gain) = weights
    nseq, t, _ = x.shape
    m = nseq * t
    x = x.reshape(m, D_MODEL)
    if caches is not None:
        cache_k, cache_v, state_dn, state_dn_conv, state_cf_conv = caches
        n_even = cache_k.shape[0]
        ck = cache_k.reshape(n_even, nseq * WINDOW, D_KV_A)
        cv = cache_v.reshape(n_even, nseq * WINDOW, D_KV_A)
    new_k, new_v, new_s, new_qkv, new_u = [], [], [], [], []
    for layer in range(DEPTH):
        i = layer // 2
        last = layer == DEPTH - 1
        if layer % 2 == 0:
            u = rms_matmul(x, norm_gain[layer], w_in_even[0], w_in_even[1], i, tm_in, TN_EVEN)
            u3 = u.reshape(nseq, t, N_EVEN_PAD)
            if caches is None:
                nw = t // WINDOW
                prev_spec = lambda col: pl.BlockSpec(
                    (WINDOW, D_KV_A), lambda n, j, col=col: (n * nw + jnp.maximum(nsub * j - 1, 0), col))
                o_a = attention(u, attn_sinks[i], u, u, prev_spec, nseq, t, tq, nsub, True)
                dn_state = None
            else:
                prev_spec = lambda col, i=i: pl.BlockSpec((None, WINDOW, D_KV_A), lambda n, j: (i, n, 0))
                o_a = attention(u, attn_sinks[i], ck, cv, prev_spec, nseq, t, tq, nsub, False)
                dn_state = (state_dn_conv, state_dn)
            o_b, s_new = deltanet(u3, dn_state, i, dn_conv_w[i], dn_a_log[i], dn_dt_bias[i], dn_norm_gain[i], dn_sb)
            x = matmul_residual([o_a, o_b.reshape(m, D_B)], w_out_even, i, x, tm_out)
            keep = min(t, WINDOW)
            heads = lambda z: z.reshape(z.shape[:2] + (N_KV_A, HD_A))
            new_k.append(heads(u3[:, t - keep:, COL_KA:COL_KA + D_KV_A]))
            new_v.append(heads(u3[:, t - keep:, COL_VA:COL_VA + D_KV_A]))
            new_s.append(s_new)
            new_qkv.append(u3[:, t - min(t, DN_CONV - 1):, COL_QKV_B:COL_QKV_B + D_QKV_B])
        else:
            uu, sz = rms_matmul_glu(x, norm_gain[layer], w_in_odd, i, tm_in, TN_ODD)
            prev_all = None if caches is None else state_cf_conv
            y = conformer_conv(uu, sz, prev_all, i, cf_conv_w[i], cf_conv_b[i], cf_ln_gain[i], cf_ln_bias[i],
                               nseq, t, tt)
            x = matmul_residual([y], w_out_odd, i, x, tm_out, final_norm_gain if last else None)
            new_u.append(uu.reshape(nseq, t, D_C)[:, t - min(t, CF_CONV - 1):])

    def with_history(old, new, length):
        new = jnp.stack(new)
        if new.shape[2] >= length:
            return new[:, :, new.shape[2] - length:]
        if old is None:
            old = jnp.zeros(new.shape[:2] + (length,) + new.shape[3:], new.dtype)
        return jnp.concatenate([old[:, :, new.shape[2]:], new], axis=2)

    old_k = old_v = old_dnc = old_cfc = None
    if caches is not None:
        old_k, old_v, old_dnc, old_cfc = cache_k, cache_v, state_dn_conv, state_cf_conv
    states = (with_history(old_k, new_k, WINDOW), with_history(old_v, new_v, WINDOW),
              jnp.stack(new_s), with_history(old_dnc, new_qkv, DN_CONV - 1),
              with_history(old_cfc, new_u, CF_CONV - 1))
    return x.reshape(nseq, t, D_MODEL), states


def kernel(x_prompt, x_sample, cache_win_k, cache_win_v, state_dn, state_dn_conv, state_cf_conv, norm_gain, w_in_even, w_out_even, attn_sinks, dn_conv_w, dn_a_log, dn_dt_bias, dn_norm_gain, w_in_odd, w_out_odd, cf_conv_w, cf_conv_b, cf_ln_gain, cf_ln_bias, final_norm_gain):
    weights = (norm_gain, _prep_even_weight(w_in_even), w_out_even.astype(BF16), attn_sinks, dn_conv_w, dn_a_log,
               dn_dt_bias, dn_norm_gain, w_in_odd.astype(BF16), w_out_odd.astype(BF16), cf_conv_w, cf_conv_b,
               cf_ln_gain, cf_ln_bias, final_norm_gain)
    y_prompt, (p_k, p_v, p_dn, p_dnc, p_cfc) = _trunk(
        x_prompt, None, weights, tm_in=1024, tm_out=512, tq=WINDOW, nsub=2, dn_sb=2, tt=512)
    caches = (cache_win_k, cache_win_v, state_dn, state_dn_conv, state_cf_conv)
    y_sample, (s_k, s_v, s_dn, s_dnc, s_cfc) = _trunk(
        x_sample, caches, weights, tm_in=256, tm_out=256, tq=x_sample.shape[1], nsub=1, dn_sb=4,
        tt=x_sample.shape[1])
    return (y_prompt, y_sample, p_k, p_v, p_dn, p_dnc, p_cfc, s_k, s_v, s_dn, s_dnc, s_cfc)
```

```python
import functools

import jax
import jax.numpy as jnp
from jax import lax
from jax.experimental import pallas as pl
from jax.experimental.pallas import tpu as pltpu

F32 = jnp.float32
BF16 = jnp.bfloat16

D_MODEL = 2048
DEPTH = 4
HD_A = 64
N_KV_A = 4
GROUP_A = 4
WINDOW = 128
D_A = 1024
D_KV_A = N_KV_A * HD_A
DK_B = 128
DV_B = 128
H_B = 8
D_B = 1024
D_QKV_B = 3072
DN_CONV = 4
DN_CHUNK = 64
D_C = 2048
CF_CONV = 31
EPS = 1e-6

COL_QA = 0
COL_KA = 1024
COL_VA = 1280
COL_ZA = 1536
COL_QKV_B = 2560
N_EVEN_MAIN = 5632
N_EVEN = 6672
TN_EVEN = 512
TAIL_ZB = 0
TAIL_AB = 1024
N_EVEN_TAIL = 1152
HALF = 512
TN_ODD = 512

LANES = 128
SUBLANES = 8
VMEM_LIMIT = 56 * 1024 * 1024


def _cparams(*sem):
    return pltpu.CompilerParams(dimension_semantics=sem, vmem_limit_bytes=VMEM_LIMIT)


def _act_dtype(block_rows):
    return BF16 if block_rows % 16 == 0 else F32


def _sigmoid(x):
    return 1.0 / (1.0 + jnp.exp(-x))


def _silu(x):
    return x * _sigmoid(x)


def _dot(a, b):
    return jnp.dot(a.astype(BF16), b.astype(BF16), preferred_element_type=F32)


def _dot_nt(a, b):
    return lax.dot_general(a.astype(BF16), b.astype(BF16), (((1,), (1,)), ((), ())),
                           preferred_element_type=F32)


def _dot_tn(a, b):
    return lax.dot_general(a.astype(BF16), b.astype(BF16), (((0,), (0,)), ((), ())),
                           preferred_element_type=F32)


def _split(a):
    hi = a.astype(BF16)
    lo = (a - hi.astype(F32)).astype(BF16)
    return hi, lo


NORM_ROWS = 256


def _normalize_rows(x_ref, g_ref, h_ref):
    tm = x_ref.shape[0]
    step = min(tm, NORM_ROWS)
    for r in range(0, tm, step):
        x = x_ref[r:r + step, :]
        ms = jnp.mean(x * x, axis=-1, keepdims=True)
        h_ref[r:r + step, :] = (x * lax.rsqrt(ms + EPS) * g_ref[...]).astype(BF16)


def _dot_wt(h, wt):
    return lax.dot_general(h, wt, (((1,), (1,)), ((), ())), preferred_element_type=F32)


def _rms_mm_body(x_ref, g_ref, wa_ref, wb_ref, oa_ref, ob_ref, h_ref, *, na):
    j = pl.program_id(1)

    @pl.when(j == 0)
    def _():
        _normalize_rows(x_ref, g_ref, h_ref)

    @pl.when(j < na)
    def _():
        oa_ref[...] = _dot_wt(h_ref[...], wa_ref[...])

    @pl.when(j == na)
    def _():
        ob_ref[...] = _dot_wt(h_ref[...], wb_ref[...])


def rms_matmul(x, g, wta_all, n_main, wtb_all, li, tm, tn):
    m, k = x.shape
    na = n_main // tn
    nb = wtb_all.shape[1]
    return pl.pallas_call(
        functools.partial(_rms_mm_body, na=na),
        grid=(m // tm, na + 1),
        in_specs=[
            pl.BlockSpec((tm, k), lambda i, j: (i, 0)),
            pl.BlockSpec((1, k), lambda i, j: (0, 0)),
            pl.BlockSpec((None, tn, k), lambda i, j: (li, jnp.minimum(j, na - 1), 0)),
            pl.BlockSpec((None, nb, k), lambda i, j: (li, 0, 0)),
        ],
        out_specs=[
            pl.BlockSpec((tm, tn), lambda i, j: (i, jnp.minimum(j, na - 1))),
            pl.BlockSpec((tm, nb), lambda i, j: (i, 0)),
        ],
        out_shape=[jax.ShapeDtypeStruct((m, n_main), F32), jax.ShapeDtypeStruct((m, nb), F32)],
        scratch_shapes=[pltpu.VMEM((tm, k), BF16)],
        compiler_params=_cparams("parallel", "arbitrary"),
        name="rms_matmul",
    )(x, g.reshape(1, k), wta_all, wtb_all)


def _rms_mm_glu_body(x_ref, g_ref, wv_ref, wg_ref, wz_ref, u_ref, sz_ref, h_ref):
    @pl.when(pl.program_id(1) == 0)
    def _():
        _normalize_rows(x_ref, g_ref, h_ref)

    h = h_ref[...]
    val = jnp.dot(h, wv_ref[...], preferred_element_type=F32)
    gate = jnp.dot(h, wg_ref[...], preferred_element_type=F32)
    z = jnp.dot(h, wz_ref[...], preferred_element_type=F32)
    u_ref[...] = val * _sigmoid(gate)
    sz_ref[...] = _silu(z)


def rms_matmul_glu(x, g, w_all, li, tm, tn):
    m, k = x.shape
    n = w_all.shape[2] // 3
    nt = n // tn
    return pl.pallas_call(
        _rms_mm_glu_body,
        grid=(m // tm, nt),
        in_specs=[
            pl.BlockSpec((tm, k), lambda i, j: (i, 0)),
            pl.BlockSpec((1, k), lambda i, j: (0, 0)),
            pl.BlockSpec((None, k, tn), lambda i, j: (li, 0, j)),
            pl.BlockSpec((None, k, tn), lambda i, j: (li, 0, j + nt)),
            pl.BlockSpec((None, k, tn), lambda i, j: (li, 0, j + 2 * nt)),
        ],
        out_specs=[
            pl.BlockSpec((tm, tn), lambda i, j: (i, j)),
            pl.BlockSpec((tm, tn), lambda i, j: (i, j)),
        ],
        out_shape=[jax.ShapeDtypeStruct((m, n), F32), jax.ShapeDtypeStruct((m, n), F32)],
        scratch_shapes=[pltpu.VMEM((tm, k), BF16)],
        compiler_params=_cparams("parallel", "arbitrary"),
        name="rms_matmul_glu",
    )(x, g.reshape(1, k), w_all, w_all, w_all)


def _mm_res_body(*refs, n_in, final):
    a_refs = refs[:n_in]
    w_refs = refs[n_in:2 * n_in]
    res_ref = refs[2 * n_in]
    o_ref = refs[-1]
    acc = res_ref[...]
    for a_ref, w_ref in zip(a_refs, w_refs):
        acc = acc + jnp.dot(a_ref[...].astype(BF16), w_ref[...], preferred_element_type=F32)
    if final:
        g_ref = refs[2 * n_in + 1]
        ms = jnp.mean(acc * acc, axis=-1, keepdims=True)
        acc = acc * lax.rsqrt(ms + EPS) * g_ref[...]
    o_ref[...] = acc


def matmul_residual(acts, w_all, li, res, tm, final_gain=None):
    m, n = res.shape
    n_in = len(acts)
    kb = acts[0].shape[1]
    assert all(a.shape[1] == kb for a in acts) and n_in * kb == w_all.shape[1]
    final = final_gain is not None
    in_specs = [pl.BlockSpec((tm, kb), lambda i: (i, 0)) for _ in acts]
    in_specs += [pl.BlockSpec((None, kb, n), lambda i, p=p: (li, p, 0)) for p in range(n_in)]
    in_specs += [pl.BlockSpec((tm, n), lambda i: (i, 0))]
    args = list(acts) + [w_all] * n_in + [res]
    if final:
        in_specs += [pl.BlockSpec((1, n), lambda i: (0, 0))]
        args += [final_gain.reshape(1, n)]
    return pl.pallas_call(
        functools.partial(_mm_res_body, n_in=n_in, final=final),
        grid=(m // tm,),
        in_specs=in_specs,
        out_specs=pl.BlockSpec((tm, n), lambda i: (i, 0)),
        out_shape=jax.ShapeDtypeStruct((m, n), F32),
        compiler_params=_cparams("parallel"),
        name="matmul_residual",
    )(*args)


def _attn_body(sink_ref, q_ref, za0_ref, za1_ref, kp_ref, kc_ref, vp_ref, vc_ref, o_ref, *, tq, nsub,
               first_block_has_no_prev):
    q = q_ref[...]
    za = jnp.concatenate([za0_ref[...], za1_ref[...]], axis=1)

    def keys(prev_ref, cur_ref):
        cur = cur_ref[...]
        if tq < WINDOW:
            cur = jnp.concatenate([cur, jnp.zeros((WINDOW - tq, D_KV_A), F32)], axis=0)
        return jnp.concatenate([prev_ref[...], cur], axis=0)

    k = keys(kp_ref, kc_ref)
    v = keys(vp_ref, vc_ref)
    rows = GROUP_A * tq
    r = lax.broadcasted_iota(jnp.int32, (rows, 2 * WINDOW), 0)
    c = lax.broadcasted_iota(jnp.int32, (rows, 2 * WINDOW), 1)
    i = r & (tq - 1)
    mask = (c >= i) & (c <= i + WINDOW)
    masks = [mask] * nsub
    if first_block_has_no_prev:
        masks[0] = mask & ((c >= WINDOW) | (pl.program_id(1) > 0))
    ones_col = (lax.broadcasted_iota(jnp.int32, (2 * WINDOW, HD_A), 1) == 0).astype(F32)
    units = [(s, h) for s in range(nsub) for h in range(N_KV_A)]
    scores = []
    for s, h in units:
        base = h * GROUP_A * HD_A
        qs = jnp.concatenate([q[s * tq:(s + 1) * tq, base + g * HD_A: base + (g + 1) * HD_A]
                              for g in range(GROUP_A)], axis=0)
        scores.append(_dot_nt(qs, k[s * WINDOW:(s + 2) * WINDOW, h * HD_A:(h + 1) * HD_A]))
    exps, sink_terms = [], []
    for (s, h), sc in zip(units, scores):
        sc = jnp.where(masks[s], sc * (HD_A ** -0.5), -jnp.inf)
        sk = jnp.concatenate([jnp.full((tq, 1), sink_ref[h * GROUP_A + g], F32) for g in range(GROUP_A)], axis=0)
        m = jnp.maximum(jnp.max(sc, axis=-1, keepdims=True), sk)
        exps.append(jnp.exp(sc - m).astype(BF16))
        sink_terms.append(jnp.exp(sk - m))
    outs = [[] for _ in range(nsub)]
    for (s, h), e, st in zip(units, exps, sink_terms):
        v_ext = jnp.concatenate([v[s * WINDOW:(s + 2) * WINDOW, h * HD_A:(h + 1) * HD_A], ones_col], axis=1)
        pv = jnp.dot(e, v_ext.astype(BF16), preferred_element_type=F32)
        o = pv[:, :HD_A] / (pv[:, HD_A:HD_A + 1] + st)
        outs[s] += [o[g * tq:(g + 1) * tq] for g in range(GROUP_A)]
    o_all = jnp.concatenate([jnp.concatenate(o, axis=1) for o in outs], axis=0)
    o_ref[...] = (o_all * _silu(za)).astype(o_ref.dtype)


def attention(u, sinks, prev_k, prev_v, prev_spec, nseq, t, tq, nsub, first_block_has_no_prev):
    bq = tq * nsub
    nb = t // bq
    row = lambda n, j: n * nb + j
    return pl.pallas_call(
        functools.partial(_attn_body, tq=tq, nsub=nsub, first_block_has_no_prev=first_block_has_no_prev),
        grid=(nseq, nb),
        in_specs=[
            pl.BlockSpec(memory_space=pltpu.SMEM),
            pl.BlockSpec((bq, D_A), lambda n, j: (row(n, j), COL_QA // D_A)),
            pl.BlockSpec((bq, HALF), lambda n, j: (row(n, j), COL_ZA // HALF)),
            pl.BlockSpec((bq, HALF), lambda n, j: (row(n, j), COL_ZA // HALF + 1)),
            prev_spec(COL_KA // D_KV_A),
            pl.BlockSpec((bq, D_KV_A), lambda n, j: (row(n, j), COL_KA // D_KV_A)),
            prev_spec(COL_VA // D_KV_A),
            pl.BlockSpec((bq, D_KV_A), lambda n, j: (row(n, j), COL_VA // D_KV_A)),
        ],
        out_specs=pl.BlockSpec((bq, D_A), lambda n, j: (row(n, j), 0)),
        out_shape=jax.ShapeDtypeStruct((nseq * t, D_A), _act_dtype(bq)),
        compiler_params=_cparams("parallel", "arbitrary"),
        name="swa_attention",
    )(sinks, u, u, u, prev_k, u, prev_v, u)


DN_HALO = SUBLANES
DN_MIN_CHUNK = 16


def _softplus(x):
    return jnp.maximum(x, 0.0) + jnp.log1p(jnp.exp(-jnp.abs(x)))


def _deltanet_body(*refs, sb, tv, ch, nc, has_state):
    n_qkv = D_QKV_B // HALF
    qkv_refs = refs[:n_qkv]
    ab_ref, zb0_ref, zb1_ref = refs[n_qkv:n_qkv + 3]
    if has_state:
        cprev_ref, s0_ref = refs[n_qkv + 3:n_qkv + 5]
    cw_ref, alog_ref, dtb_ref, ng_ref, o_ref, sfin_ref, xp_ref, s_ref = refs[-8:]
    c = pl.program_id(1)
    halo = DN_HALO
    first_tap_row = halo - (DN_CONV - 1)

    @pl.when(c == 0)
    def _():
        if has_state:
            s_ref[...] = s0_ref[...]
            xp_ref[:, first_tap_row:halo, :] = cprev_ref[...]
        else:
            s_ref[...] = jnp.zeros(s_ref.shape, F32)
            xp_ref[:, 0:halo, :] = jnp.zeros((sb, halo, D_QKV_B), F32)

    ii = lax.broadcasted_iota(jnp.int32, (ch, ch), 0)
    jj = lax.broadcasted_iota(jnp.int32, (ch, ch), 1)
    lower = ii >= jj
    strict = ii > jj
    eye = (ii == jj).astype(F32)
    tril_ones = lower.astype(F32)
    sel_rows = (lax.broadcasted_iota(jnp.int32, (H_B, LANES), 0)
                == lax.broadcasted_iota(jnp.int32, (H_B, LANES), 1)).astype(F32)
    row_id = lax.broadcasted_iota(jnp.int32, (ch, 1), 0)
    cw = cw_ref[...]
    hp = lax.Precision.HIGHEST

    units = []
    for n in range(sb):
        for p, part_ref in enumerate(qkv_refs):
            xp_ref[n, halo:halo + tv, p * HALF:(p + 1) * HALF] = part_ref[n]
        if tv < ch:
            xp_ref[n, halo + tv:halo + ch, :] = jnp.zeros((ch - tv, D_QKV_B), F32)
        y = jnp.zeros((ch, D_QKV_B), F32)
        for tap in range(DN_CONV):
            off = first_tap_row + tap
            y = y + cw[tap:tap + 1, :] * xp_ref[n, off:off + ch, :]
        y = _silu(y)
        if tv < ch:
            y = jnp.where(row_id < tv, y, 0.0)
        if nc > 1:
            xp_ref[n, 0:halo, :] = xp_ref[n, tv:tv + halo, :]
        ab = ab_ref[n]
        if tv < ch:
            ab = jnp.concatenate([ab, jnp.zeros((ch - tv, LANES), F32)], axis=0)
        gpre = -jnp.exp(alog_ref[...]) * _softplus(ab + dtb_ref[...])
        if tv < ch:
            gpre = jnp.where(row_id < tv, gpre, 0.0)
        beta_all = _sigmoid(ab)
        gcum = jnp.dot(tril_ones, gpre, precision=hp, preferred_element_type=F32)
        gcum_t = lax.dot_general(sel_rows, gcum, (((1,), (1,)), ((), ())), precision=hp,
                                 preferred_element_type=F32)
        for h in range(H_B):
            qh = y[:, h * DK_B:(h + 1) * DK_B]
            kh = y[:, 1024 + h * DK_B:1024 + (h + 1) * DK_B]
            vh = y[:, 2048 + h * DV_B:2048 + (h + 1) * DV_B]
            qh = qh * lax.rsqrt(jnp.sum(qh * qh, axis=-1, keepdims=True) + EPS) * (DK_B ** -0.5)
            kh = kh * lax.rsqrt(jnp.sum(kh * kh, axis=-1, keepdims=True) + EPS)
            gc = gcum[:, h:h + 1]
            gr = gcum_t[h:h + 1, :]
            glast = gcum[tv - 1:tv, h:h + 1]
            beta = beta_all[:, H_B + h:H_B + h + 1]
            egc = jnp.exp(gc)
            kbeta = kh * beta
            units.append(dict(
                n=n, h=h, qh=qh, kh=kh, kbeta=kbeta, glast=glast,
                decay=jnp.where(lower, jnp.exp(jnp.where(lower, gc - gr, 0.0)), 0.0),
                rhs=jnp.concatenate([vh * beta, kbeta * egc], axis=1),
                qg=qh * egc, kg=kh * jnp.exp(glast - gc)))

    for u in units:
        kq = _dot_nt(jnp.concatenate([u["kbeta"], u["qh"]], axis=0), u["kh"])
        u["w"] = jnp.concatenate([-jnp.where(strict, kq[:ch] * u["decay"], 0.0), eye], axis=1)
        u["attn"] = kq[ch:] * u["decay"]

    right = lax.broadcasted_iota(jnp.int32, (ch, 2 * ch), 1) >= ch
    for _ in range(max(1, (tv - 1).bit_length())):
        prods = [_dot(u["w"][:, :ch], u["w"]) for u in units]
        for u, m in zip(units, prods):
            u["w"] = m + jnp.where(right, u["w"], 0.0)

    for u in units:
        hi, lo = _split(u["rhs"])
        sol = jnp.dot(u["w"][:, ch:].astype(BF16), jnp.concatenate([hi, lo], axis=1), preferred_element_type=F32)
        u["u"] = sol[:, 0:DV_B] + sol[:, 2 * DV_B:3 * DV_B]
        u["wk"] = sol[:, DV_B:2 * DV_B] + sol[:, 3 * DV_B:4 * DV_B]

    for u in units:
        ws_qs = _dot(jnp.concatenate([u["wk"], u["qg"]], axis=0), s_ref[u["n"], u["h"]])
        u["v_new"] = u["u"] - ws_qs[:ch]
        u["o"] = ws_qs[ch:]

    for u in units:
        u["o"] = u["o"] + _dot(u["attn"], u["v_new"])
        s_ref[u["n"], u["h"]] = s_ref[u["n"], u["h"]] * jnp.exp(u["glast"]) + _dot_tn(u["kg"], u["v_new"])

    for n in range(sb):
        outs = []
        for u in units[n * H_B:(n + 1) * H_B]:
            o = u["o"]
            outs.append(o * lax.rsqrt(jnp.mean(o * o, axis=-1, keepdims=True) + EPS) * ng_ref[...])
        o_all = jnp.concatenate(outs, axis=1)[:tv]
        zb = jnp.concatenate([zb0_ref[n], zb1_ref[n]], axis=1)
        o_ref[n] = (o_all * _silu(zb)).astype(o_ref.dtype)

    @pl.when(c == nc - 1)
    def _():
        sfin_ref[...] = s_ref[...]


def deltanet(u3, ut3, state, li, conv_w, a_log, dt_bias, norm_g, sb):
    nseq, t, _ = u3.shape
    tv = min(DN_CHUNK, t)
    nc = t // tv
    ch = DN_CHUNK if tv == DN_CHUNK else max(tv, DN_MIN_CHUNK)
    has_state = state is not None
    pad8 = lambda x: jnp.zeros((1, LANES), F32).at[0, :H_B].set(x)
    blk = lambda col: pl.BlockSpec((sb, tv, HALF), lambda s, c: (s, c, col))
    const = lambda shape: pl.BlockSpec(shape, lambda s, c: tuple(0 for _ in shape))
    in_specs = [blk(COL_QKV_B // HALF + p) for p in range(D_QKV_B // HALF)]
    args = [u3] * len(in_specs)
    in_specs += [pl.BlockSpec((sb, tv, LANES), lambda s, c: (s, c, TAIL_AB // LANES)),
                 blk(TAIL_ZB // HALF), blk(TAIL_ZB // HALF + 1)]
    args += [ut3] * 3
    if has_state:
        in_specs += [
            pl.BlockSpec((None, sb, DN_CONV - 1, D_QKV_B), lambda s, c: (li, s, 0, 0)),
            pl.BlockSpec((None, sb, H_B, DK_B, DV_B), lambda s, c: (li, s, 0, 0, 0)),
        ]
        args += list(state)
    in_specs += [const((DN_CONV, D_QKV_B)), const((1, LANES)), const((1, LANES)), const((1, DV_B))]
    args += [conv_w, pad8(a_log), pad8(dt_bias), norm_g.reshape(1, DV_B)]
    return pl.pallas_call(
        functools.partial(_deltanet_body, sb=sb, tv=tv, ch=ch, nc=nc, has_state=has_state),
        grid=(nseq // sb, nc),
        in_specs=in_specs,
        out_specs=[
            pl.BlockSpec((sb, tv, D_B), lambda s, c: (s, c, 0)),
            pl.BlockSpec((sb, H_B, DK_B, DV_B), lambda s, c: (s, 0, 0, 0)),
        ],
        out_shape=[jax.ShapeDtypeStruct((nseq, t, D_B), _act_dtype(tv)),
                   jax.ShapeDtypeStruct((nseq, H_B, DK_B, DV_B), F32)],
        scratch_shapes=[pltpu.VMEM((sb, DN_HALO + ch, D_QKV_B), F32),
                        pltpu.VMEM((sb, H_B, DK_B, DV_B), F32)],
        compiler_params=_cparams("parallel", "arbitrary"),
        name="gated_deltanet",
    )(*args)


CF_HALO = 32
CF_FIRST = CF_HALO - (CF_CONV - 1)
CF_GROUPS_AHEAD = (CF_HALO + SUBLANES) // SUBLANES


def _tree_sum(terms):
    while len(terms) > 1:
        terms = [terms[i] + terms[i + 1] for i in range(0, len(terms) - 1, 2)] + terms[len(terms) & ~1:]
    return terms[0]


def _group_rows(g):
    if isinstance(g, int):
        return pl.ds(g * SUBLANES, SUBLANES)
    return pl.ds(pl.multiple_of(g * SUBLANES, SUBLANES), SUBLANES)


def _cfconv_body(*refs, tt, has_prev):
    if has_prev:
        u_ref, sz_ref, prev_ref, cw_ref, cb_ref, lg_ref, lb_ref, o_ref, xp_ref, wb_ref, acc_ref = refs
    else:
        u_ref, sz_ref, cw_ref, cb_ref, lg_ref, lb_ref, o_ref, xp_ref, wb_ref, acc_ref = refs

    @pl.when(pl.program_id(1) == 0)
    def _():
        for tap in range(CF_CONV):
            wb_ref[tap] = jnp.broadcast_to(cw_ref[tap:tap + 1, :], (SUBLANES, D_C))
        xp_ref[0:CF_HALO, :] = jnp.zeros((CF_HALO, D_C), F32)
        if has_prev:
            xp_ref[CF_FIRST:CF_HALO, :] = prev_ref[...]
        xp_ref[CF_HALO + tt:CF_HALO + tt + SUBLANES, :] = jnp.zeros((SUBLANES, D_C), F32)

    xp_ref[CF_HALO:CF_HALO + tt, :] = u_ref[...]

    row = lax.broadcasted_iota(jnp.int32, (SUBLANES, LANES), 0)
    ngrp = tt // SUBLANES
    for ci in range(D_C // LANES):
        cols = slice(ci * LANES, (ci + 1) * LANES)

        def zgroup(g, cols=cols):
            xs = [xp_ref[_group_rows(g + a), cols] for a in range(CF_GROUPS_AHEAD)]
            zs = []
            for b in range(SUBLANES):
                terms = [wb_ref[SUBLANES * a + b - CF_FIRST, :, cols] * xs[a]
                         for a in range(CF_GROUPS_AHEAD) if 0 <= SUBLANES * a + b - CF_FIRST < CF_CONV]
                zs.append(_tree_sum(terms))
            return tuple(zs)

        def body(g, zprev, cols=cols, zgroup=zgroup):
            znext = zgroup(g + 1)
            terms = [zprev[0]] + [pltpu.roll(jnp.where(row < b, znext[b], zprev[b]), SUBLANES - b, axis=0)
                                  for b in range(1, SUBLANES)]
            acc_ref[_group_rows(g), cols] = _tree_sum(terms)
            return znext

        lax.fori_loop(0, ngrp, body, zgroup(0), unroll=4 if ngrp % 4 == 0 else 1)

    rows = 2 * SUBLANES if tt % (2 * SUBLANES) == 0 else SUBLANES

    def norm_rows(g, carry):
        sl = pl.ds(pl.multiple_of(g * rows, rows), rows)
        acc = acc_ref[sl, :] + cb_ref[...]
        xc = acc - jnp.mean(acc, axis=-1, keepdims=True)
        y = xc * lax.rsqrt(jnp.mean(xc * xc, axis=-1, keepdims=True) + EPS)
        y = y * lg_ref[...] + lb_ref[...]
        o_ref[sl, :] = (_silu(y) * sz_ref[sl, :]).astype(o_ref.dtype)
        return carry

    ngrp_norm = tt // rows
    lax.fori_loop(0, ngrp_norm, norm_rows, 0, unroll=4 if ngrp_norm % 4 == 0 else 1)
    if tt >= CF_HALO:
        xp_ref[0:CF_HALO, :] = xp_ref[tt:tt + CF_HALO, :]


def conformer_conv(u, sz, prev_all, li, conv_w, conv_b, ln_g, ln_b, nseq, t, tt):
    nb = t // tt
    assert nb == 1 or tt >= CF_HALO
    has_prev = prev_all is not None
    row = lambda n, j: (n * nb + j, 0)
    const = lambda shape: pl.BlockSpec(shape, lambda n, j: (0, 0))
    in_specs = [pl.BlockSpec((tt, D_C), row), pl.BlockSpec((tt, D_C), row)]
    args = [u, sz]
    if has_prev:
        in_specs += [pl.BlockSpec((None, None, CF_CONV - 1, D_C), lambda n, j: (li, n, 0, 0))]
        args += [prev_all]
    in_specs += [const((CF_CONV, D_C)), const((1, D_C)), const((1, D_C)), const((1, D_C))]
    args += [conv_w, conv_b.reshape(1, D_C), ln_g.reshape(1, D_C), ln_b.reshape(1, D_C)]
    return pl.pallas_call(
        functools.partial(_cfconv_body, tt=tt, has_prev=has_prev),
        grid=(nseq, nb),
        in_specs=in_specs,
        out_specs=pl.BlockSpec((tt, D_C), row),
        out_shape=jax.ShapeDtypeStruct((nseq * t, D_C), _act_dtype(tt)),
        scratch_shapes=[pltpu.VMEM((CF_HALO + tt + SUBLANES, D_C), F32),
                        pltpu.VMEM((CF_CONV, SUBLANES, D_C), F32),
                        pltpu.VMEM((tt, D_C), F32)],
        compiler_params=_cparams("parallel", "arbitrary"),
        name="conformer_conv",
    )(*args)


def _prep_even_weight(w):
    wt = jnp.swapaxes(w, 1, 2).astype(BF16)
    ab_end = N_EVEN_MAIN + 2 * H_B
    tail = [wt[:, ab_end:N_EVEN], wt[:, N_EVEN_MAIN:ab_end],
            jnp.zeros((wt.shape[0], N_EVEN_TAIL - (N_EVEN - N_EVEN_MAIN), wt.shape[2]), BF16)]
    return wt, jnp.concatenate(tail, axis=1)


def _trunk(x, caches, weights, tm_in, tm_out, tq, nsub, dn_sb, tt):
    (norm_gain, w_in_even, w_out_even, attn_sinks, dn_conv_w, dn_a_log, dn_dt_bias, dn_norm_gain,
     w_in_odd, w_out_odd, cf_conv_w, cf_conv_b, cf_ln_gain, cf_ln_bias, final_norm_gain) = weights
    nseq, t, _ = x.shape
    m = nseq * t
    x = x.reshape(m, D_MODEL)
    if caches is not None:
        cache_k, cache_v, state_dn, state_dn_conv, state_cf_conv = caches
        n_even = cache_k.shape[0]
        ck = cache_k.reshape(n_even, nseq * WINDOW, D_KV_A)
        cv = cache_v.reshape(n_even, nseq * WINDOW, D_KV_A)
    new_k, new_v, new_s, new_qkv, new_u = [], [], [], [], []
    for layer in range(DEPTH):
        i = layer // 2
        last = layer == DEPTH - 1
        if layer % 2 == 0:
            u, ut = rms_matmul(x, norm_gain[layer], w_in_even[0], N_EVEN_MAIN, w_in_even[1], i, tm_in, TN_EVEN)
            u3 = u.reshape(nseq, t, N_EVEN_MAIN)
            ut3 = ut.reshape(nseq, t, N_EVEN_TAIL)
            if caches is None:
                nw = t // WINDOW
                prev_spec = lambda col: pl.BlockSpec(
                    (WINDOW, D_KV_A), lambda n, j, col=col: (n * nw + jnp.maximum(nsub * j - 1, 0), col))
                o_a = attention(u, attn_sinks[i], u, u, prev_spec, nseq, t, tq, nsub, True)
                dn_state = None
            else:
                prev_spec = lambda col, i=i: pl.BlockSpec((None, WINDOW, D_KV_A), lambda n, j: (i, n, 0))
                o_a = attention(u, attn_sinks[i], ck, cv, prev_spec, nseq, t, tq, nsub, False)
                dn_state = (state_dn_conv, state_dn)
            o_b, s_new = deltanet(u3, ut3, dn_state, i, dn_conv_w[i], dn_a_log[i], dn_dt_bias[i], dn_norm_gain[i], dn_sb)
            x = matmul_residual([o_a, o_b.reshape(m, D_B)], w_out_even, i, x, tm_out)
            keep = min(t, WINDOW)
            heads = lambda z: z.reshape(z.shape[:2] + (N_KV_A, HD_A))
            new_k.append(heads(u3[:, t - keep:, COL_KA:COL_KA + D_KV_A]))
            new_v.append(heads(u3[:, t - keep:, COL_VA:COL_VA + D_KV_A]))
            new_s.append(s_new)
            new_qkv.append(u3[:, t - min(t, DN_CONV - 1):, COL_QKV_B:COL_QKV_B + D_QKV_B])
        else:
            uu, sz = rms_matmul_glu(x, norm_gain[layer], w_in_odd, i, tm_in, TN_ODD)
            prev_all = None if caches is None else state_cf_conv
            y = conformer_conv(uu, sz, prev_all, i, cf_conv_w[i], cf_conv_b[i], cf_ln_gain[i], cf_ln_bias[i],
                               nseq, t, tt)
            x = matmul_residual([y], w_out_odd, i, x, tm_out, final_norm_gain if last else None)
            new_u.append(uu.reshape(nseq, t, D_C)[:, t - min(t, CF_CONV - 1):])

    def with_history(old, new, length):
        new = jnp.stack(new)
        if new.shape[2] >= length:
            return new[:, :, new.shape[2] - length:]
        if old is None:
            old = jnp.zeros(new.shape[:2] + (length,) + new.shape[3:], new.dtype)
        return jnp.concatenate([old[:, :, new.shape[2]:], new], axis=2)

    old_k = old_v = old_dnc = old_cfc = None
    if caches is not None:
        old_k, old_v, old_dnc, old_cfc = cache_k, cache_v, state_dn_conv, state_cf_conv
    states = (with_history(old_k, new_k, WINDOW), with_history(old_v, new_v, WINDOW),
              jnp.stack(new_s), with_history(old_dnc, new_qkv, DN_CONV - 1),
              with_history(old_cfc, new_u, CF_CONV - 1))
    return x.reshape(nseq, t, D_MODEL), states


def kernel(x_prompt, x_sample, cache_win_k, cache_win_v, state_dn, state_dn_conv, state_cf_conv, norm_gain, w_in_even, w_out_even, attn_sinks, dn_conv_w, dn_a_log, dn_dt_bias, dn_norm_gain, w_in_odd, w_out_odd, cf_conv_w, cf_conv_b, cf_ln_gain, cf_ln_bias, final_norm_gain):
    weights = (norm_gain, _prep_even_weight(w_in_even), w_out_even.astype(BF16), attn_sinks, dn_conv_w, dn_a_log,
               dn_dt_bias, dn_norm_gain, w_in_odd.astype(BF16), w_out_odd.astype(BF16), cf_conv_w, cf_conv_b,
               cf_ln_gain, cf_ln_bias, final_norm_gain)
    y_prompt, (p_k, p_v, p_dn, p_dnc, p_cfc) = _trunk(
        x_prompt, None, weights, tm_in=1024, tm_out=512, tq=WINDOW, nsub=2, dn_sb=2, tt=512)
    caches = (cache_win_k, cache_win_v, state_dn, state_dn_conv, state_cf_conv)
    y_sample, (s_k, s_v, s_dn, s_dnc, s_cfc) = _trunk(
        x_sample, caches, weights, tm_in=256, tm_out=256, tq=x_sample.shape[1], nsub=1, dn_sb=4,
        tt=x_sample.shape[1])
    return (y_prompt, y_sample, p_k, p_v, p_dn, p_dnc, p_cfc, s_k, s_v, s_dn, s_dnc, s_cfc)
```

```python
import functools

import jax
import jax.numpy as jnp
from jax import lax
from jax.experimental import pallas as pl
from jax.experimental.pallas import tpu as pltpu

F32 = jnp.float32
BF16 = jnp.bfloat16

D_MODEL = 2048
DEPTH = 4
HD_A = 64
N_KV_A = 4
GROUP_A = 4
WINDOW = 128
D_A = 1024
D_KV_A = N_KV_A * HD_A
DK_B = 128
DV_B = 128
H_B = 8
D_B = 1024
D_QKV_B = 3072
DN_CONV = 4
DN_CHUNK = 64
D_C = 2048
CF_CONV = 31
EPS = 1e-6

COL_QA = 0
COL_KA = 1024
COL_VA = 1280
COL_ZA = 1536
COL_QKV_B = 2560
N_EVEN_MAIN = 5632
N_EVEN = 6672
TN_EVEN = 512
TAIL_ZB = 0
TAIL_AB = 1024
N_EVEN_TAIL = 1152
HALF = 512
TN_ODD = 512

LANES = 128
SUBLANES = 8
VMEM_LIMIT = 56 * 1024 * 1024


def _cparams(*sem):
    return pltpu.CompilerParams(dimension_semantics=sem, vmem_limit_bytes=VMEM_LIMIT)


def _act_dtype(block_rows):
    return BF16 if block_rows % 16 == 0 else F32


def _sigmoid(x):
    return 1.0 / (1.0 + jnp.exp(-x))


def _silu(x):
    return x * _sigmoid(x)


def _dot(a, b):
    return jnp.dot(a.astype(BF16), b.astype(BF16), preferred_element_type=F32)


def _dot_nt(a, b):
    return lax.dot_general(a.astype(BF16), b.astype(BF16), (((1,), (1,)), ((), ())),
                           preferred_element_type=F32)


def _dot_tn(a, b):
    return lax.dot_general(a.astype(BF16), b.astype(BF16), (((0,), (0,)), ((), ())),
                           preferred_element_type=F32)


def _split(a):
    hi = a.astype(BF16)
    lo = (a - hi.astype(F32)).astype(BF16)
    return hi, lo


NORM_ROWS = 256


def _normalize_rows(x_ref, g_ref, h_ref):
    tm = x_ref.shape[0]
    step = min(tm, NORM_ROWS)
    for r in range(0, tm, step):
        x = x_ref[r:r + step, :]
        ms = jnp.mean(x * x, axis=-1, keepdims=True)
        h_ref[r:r + step, :] = (x * lax.rsqrt(ms + EPS) * g_ref[...]).astype(BF16)


def _dot_wt(h, wt):
    return lax.dot_general(h, wt, (((1,), (1,)), ((), ())), preferred_element_type=F32)


def _rms_mm_body(x_ref, g_ref, wa_ref, wb_ref, oa_ref, ob_ref, h_ref, *, na):
    j = pl.program_id(1)

    @pl.when(j == 0)
    def _():
        _normalize_rows(x_ref, g_ref, h_ref)

    @pl.when(j < na)
    def _():
        oa_ref[...] = _dot_wt(h_ref[...], wa_ref[...])

    @pl.when(j == na)
    def _():
        ob_ref[...] = _dot_wt(h_ref[...], wb_ref[...])


def rms_matmul(x, g, wta_all, n_main, wtb_all, li, tm, tn):
    m, k = x.shape
    na = n_main // tn
    nb = wtb_all.shape[1]
    return pl.pallas_call(
        functools.partial(_rms_mm_body, na=na),
        grid=(m // tm, na + 1),
        in_specs=[
            pl.BlockSpec((tm, k), lambda i, j: (i, 0)),
            pl.BlockSpec((1, k), lambda i, j: (0, 0)),
            pl.BlockSpec((None, tn, k), lambda i, j: (li, jnp.minimum(j, na - 1), 0)),
            pl.BlockSpec((None, nb, k), lambda i, j: (li, 0, 0)),
        ],
        out_specs=[
            pl.BlockSpec((tm, tn), lambda i, j: (i, jnp.minimum(j, na - 1))),
            pl.BlockSpec((tm, nb), lambda i, j: (i, 0)),
        ],
        out_shape=[jax.ShapeDtypeStruct((m, n_main), F32), jax.ShapeDtypeStruct((m, nb), F32)],
        scratch_shapes=[pltpu.VMEM((tm, k), BF16)],
        compiler_params=_cparams("parallel", "arbitrary"),
        name="rms_matmul",
    )(x, g.reshape(1, k), wta_all, wtb_all)


def _rms_mm_glu_body(x_ref, g_ref, wv_ref, wg_ref, wz_ref, u_ref, sz_ref, h_ref):
    @pl.when(pl.program_id(1) == 0)
    def _():
        _normalize_rows(x_ref, g_ref, h_ref)

    h = h_ref[...]
    val = jnp.dot(h, wv_ref[...], preferred_element_type=F32)
    gate = jnp.dot(h, wg_ref[...], preferred_element_type=F32)
    z = jnp.dot(h, wz_ref[...], preferred_element_type=F32)
    u_ref[...] = val * _sigmoid(gate)
    sz_ref[...] = _silu(z)


def rms_matmul_glu(x, g, w_all, li, tm, tn):
    m, k = x.shape
    n = w_all.shape[2] // 3
    nt = n // tn
    return pl.pallas_call(
        _rms_mm_glu_body,
        grid=(m // tm, nt),
        in_specs=[
            pl.BlockSpec((tm, k), lambda i, j: (i, 0)),
            pl.BlockSpec((1, k), lambda i, j: (0, 0)),
            pl.BlockSpec((None, k, tn), lambda i, j: (li, 0, j)),
            pl.BlockSpec((None, k, tn), lambda i, j: (li, 0, j + nt)),
            pl.BlockSpec((None, k, tn), lambda i, j: (li, 0, j + 2 * nt)),
        ],
        out_specs=[
            pl.BlockSpec((tm, tn), lambda i, j: (i, j)),
            pl.BlockSpec((tm, tn), lambda i, j: (i, j)),
        ],
        out_shape=[jax.ShapeDtypeStruct((m, n), F32), jax.ShapeDtypeStruct((m, n), F32)],
        scratch_shapes=[pltpu.VMEM((tm, k), BF16)],
        compiler_params=_cparams("parallel", "arbitrary"),
        name="rms_matmul_glu",
    )(x, g.reshape(1, k), w_all, w_all, w_all)


def _mm_res_body(*refs, n_in, final):
    a_refs = refs[:n_in]
    w_refs = refs[n_in:2 * n_in]
    res_ref = refs[2 * n_in]
    o_ref = refs[-1]
    acc = res_ref[...]
    for a_ref, w_ref in zip(a_refs, w_refs):
        acc = acc + jnp.dot(a_ref[...].astype(BF16), w_ref[...], preferred_element_type=F32)
    if final:
        g_ref = refs[2 * n_in + 1]
        ms = jnp.mean(acc * acc, axis=-1, keepdims=True)
        acc = acc * lax.rsqrt(ms + EPS) * g_ref[...]
    o_ref[...] = acc


def matmul_residual(acts, w_all, li, res, tm, final_gain=None):
    m, n = res.shape
    n_in = len(acts)
    kb = acts[0].shape[1]
    assert all(a.shape[1] == kb for a in acts) and n_in * kb == w_all.shape[1]
    final = final_gain is not None
    in_specs = [pl.BlockSpec((tm, kb), lambda i: (i, 0)) for _ in acts]
    in_specs += [pl.BlockSpec((None, kb, n), lambda i, p=p: (li, p, 0)) for p in range(n_in)]
    in_specs += [pl.BlockSpec((tm, n), lambda i: (i, 0))]
    args = list(acts) + [w_all] * n_in + [res]
    if final:
        in_specs += [pl.BlockSpec((1, n), lambda i: (0, 0))]
        args += [final_gain.reshape(1, n)]
    return pl.pallas_call(
        functools.partial(_mm_res_body, n_in=n_in, final=final),
        grid=(m // tm,),
        in_specs=in_specs,
        out_specs=pl.BlockSpec((tm, n), lambda i: (i, 0)),
        out_shape=jax.ShapeDtypeStruct((m, n), F32),
        compiler_params=_cparams("parallel"),
        name="matmul_residual",
    )(*args)


def _attn_body(sink_ref, q_ref, za0_ref, za1_ref, kp_ref, kc_ref, vp_ref, vc_ref, o_ref, *, tq, nsub,
               first_block_has_no_prev):
    q = q_ref[...]
    za = jnp.concatenate([za0_ref[...], za1_ref[...]], axis=1)

    def keys(prev_ref, cur_ref):
        prev, cur = prev_ref[...], cur_ref[...]
        if tq == WINDOW:
            rows = jnp.concatenate([prev, cur], axis=0)
            return [rows[s * WINDOW:(s + 2) * WINDOW] for s in range(nsub)]
        pad = jnp.zeros((WINDOW - tq, D_KV_A), F32)
        return [jnp.concatenate([prev[s * WINDOW:(s + 1) * WINDOW], cur[s * tq:(s + 1) * tq], pad], axis=0)
                for s in range(nsub)]

    k = keys(kp_ref, kc_ref)
    v = keys(vp_ref, vc_ref)
    rows = GROUP_A * tq
    r = lax.broadcasted_iota(jnp.int32, (rows, 2 * WINDOW), 0)
    c = lax.broadcasted_iota(jnp.int32, (rows, 2 * WINDOW), 1)
    i = r & (tq - 1)
    mask = (c >= i) & (c <= i + WINDOW)
    masks = [mask] * nsub
    if first_block_has_no_prev:
        masks[0] = mask & ((c >= WINDOW) | (pl.program_id(1) > 0))
    ones_col = (lax.broadcasted_iota(jnp.int32, (2 * WINDOW, HD_A), 1) == 0).astype(F32)
    units = [(s, h) for s in range(nsub) for h in range(N_KV_A)]
    scores = []
    for s, h in units:
        base = h * GROUP_A * HD_A
        qs = jnp.concatenate([q[s * tq:(s + 1) * tq, base + g * HD_A: base + (g + 1) * HD_A]
                              for g in range(GROUP_A)], axis=0)
        scores.append(_dot_nt(qs, k[s][:, h * HD_A:(h + 1) * HD_A]))
    exps, sink_terms = [], []
    for (s, h), sc in zip(units, scores):
        sc = jnp.where(masks[s], sc * (HD_A ** -0.5), -jnp.inf)
        sk = jnp.concatenate([jnp.full((tq, 1), sink_ref[h * GROUP_A + g], F32) for g in range(GROUP_A)], axis=0)
        m = jnp.maximum(jnp.max(sc, axis=-1, keepdims=True), sk)
        exps.append(jnp.exp(sc - m).astype(BF16))
        sink_terms.append(jnp.exp(sk - m))
    outs = [[] for _ in range(nsub)]
    for (s, h), e, st in zip(units, exps, sink_terms):
        v_ext = jnp.concatenate([v[s][:, h * HD_A:(h + 1) * HD_A], ones_col], axis=1)
        pv = jnp.dot(e, v_ext.astype(BF16), preferred_element_type=F32)
        o = pv[:, :HD_A] / (pv[:, HD_A:HD_A + 1] + st)
        outs[s] += [o[g * tq:(g + 1) * tq] for g in range(GROUP_A)]
    o_all = jnp.concatenate([jnp.concatenate(o, axis=1) for o in outs], axis=0)
    o_ref[...] = (o_all * _silu(za)).astype(o_ref.dtype)


def attention(u, sinks, prev_k, prev_v, prev_spec, nseq, t, tq, nsub, first_block_has_no_prev):
    bq = tq * nsub
    nb = max(1, t // bq)
    groups = nseq * t // (bq * nb)
    row = lambda n, j: n * nb + j
    return pl.pallas_call(
        functools.partial(_attn_body, tq=tq, nsub=nsub, first_block_has_no_prev=first_block_has_no_prev),
        grid=(groups, nb),
        in_specs=[
            pl.BlockSpec(memory_space=pltpu.SMEM),
            pl.BlockSpec((bq, D_A), lambda n, j: (row(n, j), COL_QA // D_A)),
            pl.BlockSpec((bq, HALF), lambda n, j: (row(n, j), COL_ZA // HALF)),
            pl.BlockSpec((bq, HALF), lambda n, j: (row(n, j), COL_ZA // HALF + 1)),
            prev_spec(COL_KA // D_KV_A),
            pl.BlockSpec((bq, D_KV_A), lambda n, j: (row(n, j), COL_KA // D_KV_A)),
            prev_spec(COL_VA // D_KV_A),
            pl.BlockSpec((bq, D_KV_A), lambda n, j: (row(n, j), COL_VA // D_KV_A)),
        ],
        out_specs=pl.BlockSpec((bq, D_A), lambda n, j: (row(n, j), 0)),
        out_shape=jax.ShapeDtypeStruct((nseq * t, D_A), _act_dtype(bq)),
        compiler_params=_cparams("parallel", "arbitrary"),
        name="swa_attention",
    )(sinks, u, u, u, prev_k, u, prev_v, u)


DN_HALO = SUBLANES
DN_MIN_CHUNK = 16


def _softplus(x):
    return jnp.maximum(x, 0.0) + jnp.log1p(jnp.exp(-jnp.abs(x)))


def _deltanet_body(*refs, sb, tv, ch, nc, has_state):
    n_qkv = D_QKV_B // HALF
    qkv_refs = refs[:n_qkv]
    ab_ref, zb0_ref, zb1_ref = refs[n_qkv:n_qkv + 3]
    if has_state:
        cprev_ref, s0_ref = refs[n_qkv + 3:n_qkv + 5]
    cw_ref, alog_ref, dtb_ref, ng_ref, o_ref, sfin_ref, xp_ref, s_ref = refs[-8:]
    c = pl.program_id(1)
    halo = DN_HALO
    first_tap_row = halo - (DN_CONV - 1)

    @pl.when(c == 0)
    def _():
        if has_state:
            s_ref[...] = s0_ref[...]
            xp_ref[:, first_tap_row:halo, :] = cprev_ref[...]
        else:
            s_ref[...] = jnp.zeros(s_ref.shape, F32)
            xp_ref[:, 0:halo, :] = jnp.zeros((sb, halo, D_QKV_B), F32)

    ii = lax.broadcasted_iota(jnp.int32, (ch, ch), 0)
    jj = lax.broadcasted_iota(jnp.int32, (ch, ch), 1)
    lower = ii >= jj
    strict = ii > jj
    eye = (ii == jj).astype(F32)
    tril_ones = lower.astype(F32)
    sel_rows = (lax.broadcasted_iota(jnp.int32, (H_B, LANES), 0)
                == lax.broadcasted_iota(jnp.int32, (H_B, LANES), 1)).astype(F32)
    row_id = lax.broadcasted_iota(jnp.int32, (ch, 1), 0)
    cw = cw_ref[...]
    hp = lax.Precision.HIGHEST

    units = []
    for n in range(sb):
        for p, part_ref in enumerate(qkv_refs):
            xp_ref[n, halo:halo + tv, p * HALF:(p + 1) * HALF] = part_ref[n]
        if tv < ch:
            xp_ref[n, halo + tv:halo + ch, :] = jnp.zeros((ch - tv, D_QKV_B), F32)
        y = jnp.zeros((ch, D_QKV_B), F32)
        for tap in range(DN_CONV):
            off = first_tap_row + tap
            y = y + cw[tap:tap + 1, :] * xp_ref[n, off:off + ch, :]
        y = _silu(y)
        if tv < ch:
            y = jnp.where(row_id < tv, y, 0.0)
        if nc > 1:
            xp_ref[n, 0:halo, :] = xp_ref[n, tv:tv + halo, :]
        ab = ab_ref[n]
        if tv < ch:
            ab = jnp.concatenate([ab, jnp.zeros((ch - tv, LANES), F32)], axis=0)
        gpre = -jnp.exp(alog_ref[...]) * _softplus(ab + dtb_ref[...])
        if tv < ch:
            gpre = jnp.where(row_id < tv, gpre, 0.0)
        beta_all = _sigmoid(ab)
        gcum = jnp.dot(tril_ones, gpre, precision=hp, preferred_element_type=F32)
        gcum_t = lax.dot_general(sel_rows, gcum, (((1,), (1,)), ((), ())), precision=hp,
                                 preferred_element_type=F32)
        for h in range(H_B):
            qh = y[:, h * DK_B:(h + 1) * DK_B]
            kh = y[:, 1024 + h * DK_B:1024 + (h + 1) * DK_B]
            vh = y[:, 2048 + h * DV_B:2048 + (h + 1) * DV_B]
            qh = qh * lax.rsqrt(jnp.sum(qh * qh, axis=-1, keepdims=True) + EPS) * (DK_B ** -0.5)
            kh = kh * lax.rsqrt(jnp.sum(kh * kh, axis=-1, keepdims=True) + EPS)
            gc = gcum[:, h:h + 1]
            gr = gcum_t[h:h + 1, :]
            glast = gcum[tv - 1:tv, h:h + 1]
            beta = beta_all[:, H_B + h:H_B + h + 1]
            egc = jnp.exp(gc)
            kbeta = kh * beta
            units.append(dict(
                n=n, h=h, qh=qh, kh=kh, kbeta=kbeta, glast=glast,
                decay=jnp.where(lower, jnp.exp(jnp.where(lower, gc - gr, 0.0)), 0.0),
                rhs=jnp.concatenate([vh * beta, kbeta * egc], axis=1),
                qg=qh * egc, kg=kh * jnp.exp(glast - gc)))

    for u in units:
        kq = _dot_nt(jnp.concatenate([u["kbeta"], u["qh"]], axis=0), u["kh"])
        u["w"] = jnp.concatenate([-jnp.where(strict, kq[:ch] * u["decay"], 0.0), eye], axis=1)
        u["attn"] = kq[ch:] * u["decay"]

    right = lax.broadcasted_iota(jnp.int32, (ch, 2 * ch), 1) >= ch
    for _ in range(max(1, (tv - 1).bit_length())):
        prods = [_dot(u["w"][:, :ch], u["w"]) for u in units]
        for u, m in zip(units, prods):
            u["w"] = m + jnp.where(right, u["w"], 0.0)

    for u in units:
        hi, lo = _split(u["rhs"])
        sol = jnp.dot(u["w"][:, ch:].astype(BF16), jnp.concatenate([hi, lo], axis=1), preferred_element_type=F32)
        u["u"] = sol[:, 0:DV_B] + sol[:, 2 * DV_B:3 * DV_B]
        u["wk"] = sol[:, DV_B:2 * DV_B] + sol[:, 3 * DV_B:4 * DV_B]

    for u in units:
        ws_qs = _dot(jnp.concatenate([u["wk"], u["qg"]], axis=0), s_ref[u["n"], u["h"]])
        u["v_new"] = u["u"] - ws_qs[:ch]
        u["o"] = ws_qs[ch:]

    for u in units:
        u["o"] = u["o"] + _dot(u["attn"], u["v_new"])
        s_ref[u["n"], u["h"]] = s_ref[u["n"], u["h"]] * jnp.exp(u["glast"]) + _dot_tn(u["kg"], u["v_new"])

    for n in range(sb):
        outs = []
        for u in units[n * H_B:(n + 1) * H_B]:
            o = u["o"]
            outs.append(o * lax.rsqrt(jnp.mean(o * o, axis=-1, keepdims=True) + EPS) * ng_ref[...])
        o_all = jnp.concatenate(outs, axis=1)[:tv]
        zb = jnp.concatenate([zb0_ref[n], zb1_ref[n]], axis=1)
        o_ref[n] = (o_all * _silu(zb)).astype(o_ref.dtype)

    @pl.when(c == nc - 1)
    def _():
        sfin_ref[...] = s_ref[...]


def deltanet(u3, ut3, state, li, conv_w, a_log, dt_bias, norm_g, sb):
    nseq, t, _ = u3.shape
    tv = min(DN_CHUNK, t)
    nc = t // tv
    ch = DN_CHUNK if tv == DN_CHUNK else max(tv, DN_MIN_CHUNK)
    has_state = state is not None
    pad8 = lambda x: jnp.zeros((1, LANES), F32).at[0, :H_B].set(x)
    blk = lambda col: pl.BlockSpec((sb, tv, HALF), lambda s, c: (s, c, col))
    const = lambda shape: pl.BlockSpec(shape, lambda s, c: tuple(0 for _ in shape))
    in_specs = [blk(COL_QKV_B // HALF + p) for p in range(D_QKV_B // HALF)]
    args = [u3] * len(in_specs)
    in_specs += [pl.BlockSpec((sb, tv, LANES), lambda s, c: (s, c, TAIL_AB // LANES)),
                 blk(TAIL_ZB // HALF), blk(TAIL_ZB // HALF + 1)]
    args += [ut3] * 3
    if has_state:
        in_specs += [
            pl.BlockSpec((None, sb, DN_CONV - 1, D_QKV_B), lambda s, c: (li, s, 0, 0)),
            pl.BlockSpec((None, sb, H_B, DK_B, DV_B), lambda s, c: (li, s, 0, 0, 0)),
        ]
        args += list(state)
    in_specs += [const((DN_CONV, D_QKV_B)), const((1, LANES)), const((1, LANES)), const((1, DV_B))]
    args += [conv_w, pad8(a_log), pad8(dt_bias), norm_g.reshape(1, DV_B)]
    return pl.pallas_call(
        functools.partial(_deltanet_body, sb=sb, tv=tv, ch=ch, nc=nc, has_state=has_state),
        grid=(nseq // sb, nc),
        in_specs=in_specs,
        out_specs=[
            pl.BlockSpec((sb, tv, D_B), lambda s, c: (s, c, 0)),
            pl.BlockSpec((sb, H_B, DK_B, DV_B), lambda s, c: (s, 0, 0, 0)),
        ],
        out_shape=[jax.ShapeDtypeStruct((nseq, t, D_B), _act_dtype(tv)),
                   jax.ShapeDtypeStruct((nseq, H_B, DK_B, DV_B), F32)],
        scratch_shapes=[pltpu.VMEM((sb, DN_HALO + ch, D_QKV_B), F32),
                        pltpu.VMEM((sb, H_B, DK_B, DV_B), F32)],
        compiler_params=_cparams("parallel", "arbitrary"),
        name="gated_deltanet",
    )(*args)


CF_HALO = 32
CF_FIRST = CF_HALO - (CF_CONV - 1)
CF_GROUPS_AHEAD = (CF_HALO + SUBLANES) // SUBLANES


def _tree_sum(terms):
    while len(terms) > 1:
        terms = [terms[i] + terms[i + 1] for i in range(0, len(terms) - 1, 2)] + terms[len(terms) & ~1:]
    return terms[0]


def _group_rows(g):
    if isinstance(g, int):
        return pl.ds(g * SUBLANES, SUBLANES)
    return pl.ds(pl.multiple_of(g * SUBLANES, SUBLANES), SUBLANES)


def _cfconv_body(*refs, tt, has_prev):
    if has_prev:
        u_ref, sz_ref, prev_ref, cw_ref, cb_ref, lg_ref, lb_ref, o_ref, xp_ref, wb_ref, acc_ref = refs
    else:
        u_ref, sz_ref, cw_ref, cb_ref, lg_ref, lb_ref, o_ref, xp_ref, wb_ref, acc_ref = refs

    @pl.when((pl.program_id(0) == 0) & (pl.program_id(1) == 0))
    def _():
        for tap in range(CF_CONV):
            wb_ref[tap] = jnp.broadcast_to(cw_ref[tap:tap + 1, :], (SUBLANES, D_C))

    @pl.when(pl.program_id(1) == 0)
    def _():
        xp_ref[0:CF_HALO, :] = jnp.zeros((CF_HALO, D_C), F32)
        if has_prev:
            xp_ref[CF_FIRST:CF_HALO, :] = prev_ref[...]
        xp_ref[CF_HALO + tt:CF_HALO + tt + SUBLANES, :] = jnp.zeros((SUBLANES, D_C), F32)

    xp_ref[CF_HALO:CF_HALO + tt, :] = u_ref[...]

    row = lax.broadcasted_iota(jnp.int32, (SUBLANES, LANES), 0)
    ngrp = tt // SUBLANES
    for ci in range(D_C // LANES):
        cols = slice(ci * LANES, (ci + 1) * LANES)

        def zgroup(g, cols=cols):
            xs = [xp_ref[_group_rows(g + a), cols] for a in range(CF_GROUPS_AHEAD)]
            zs = []
            for b in range(SUBLANES):
                terms = [wb_ref[SUBLANES * a + b - CF_FIRST, :, cols] * xs[a]
                         for a in range(CF_GROUPS_AHEAD) if 0 <= SUBLANES * a + b - CF_FIRST < CF_CONV]
                zs.append(_tree_sum(terms))
            return tuple(zs)

        def body(g, zprev, cols=cols, zgroup=zgroup):
            znext = zgroup(g + 1)
            terms = [zprev[0]] + [pltpu.roll(jnp.where(row < b, znext[b], zprev[b]), SUBLANES - b, axis=0)
                                  for b in range(1, SUBLANES)]
            acc_ref[_group_rows(g), cols] = _tree_sum(terms)
            return znext

        lax.fori_loop(0, ngrp, body, zgroup(0), unroll=4 if ngrp % 4 == 0 else 1)

    rows = 2 * SUBLANES if tt % (2 * SUBLANES) == 0 else SUBLANES

    def norm_rows(g, carry):
        sl = pl.ds(pl.multiple_of(g * rows, rows), rows)
        acc = acc_ref[sl, :] + cb_ref[...]
        xc = acc - jnp.mean(acc, axis=-1, keepdims=True)
        y = xc * lax.rsqrt(jnp.mean(xc * xc, axis=-1, keepdims=True) + EPS)
        y = y * lg_ref[...] + lb_ref[...]
        o_ref[sl, :] = (_silu(y) * sz_ref[sl, :]).astype(o_ref.dtype)
        return carry

    ngrp_norm = tt // rows
    lax.fori_loop(0, ngrp_norm, norm_rows, 0, unroll=4 if ngrp_norm % 4 == 0 else 1)
    if tt >= CF_HALO:
        xp_ref[0:CF_HALO, :] = xp_ref[tt:tt + CF_HALO, :]


def conformer_conv(u, sz, prev_all, li, conv_w, conv_b, ln_g, ln_b, nseq, t, tt):
    nb = t // tt
    assert nb == 1 or tt >= CF_HALO
    has_prev = prev_all is not None
    row = lambda n, j: (n * nb + j, 0)
    const = lambda shape: pl.BlockSpec(shape, lambda n, j: (0, 0))
    in_specs = [pl.BlockSpec((tt, D_C), row), pl.BlockSpec((tt, D_C), row)]
    args = [u, sz]
    if has_prev:
        in_specs += [pl.BlockSpec((None, None, CF_CONV - 1, D_C), lambda n, j: (li, n, 0, 0))]
        args += [prev_all]
    in_specs += [const((CF_CONV, D_C)), const((1, D_C)), const((1, D_C)), const((1, D_C))]
    args += [conv_w, conv_b.reshape(1, D_C), ln_g.reshape(1, D_C), ln_b.reshape(1, D_C)]
    return pl.pallas_call(
        functools.partial(_cfconv_body, tt=tt, has_prev=has_prev),
        grid=(nseq, nb),
        in_specs=in_specs,
        out_specs=pl.BlockSpec((tt, D_C), row),
        out_shape=jax.ShapeDtypeStruct((nseq * t, D_C), _act_dtype(tt)),
        scratch_shapes=[pltpu.VMEM((CF_HALO + tt + SUBLANES, D_C), F32),
                        pltpu.VMEM((CF_CONV, SUBLANES, D_C), F32),
                        pltpu.VMEM((tt, D_C), F32)],
        compiler_params=_cparams("arbitrary", "arbitrary"),
        name="conformer_conv",
    )(*args)


def _prep_even_weight(w):
    wt = jnp.swapaxes(w, 1, 2).astype(BF16)
    ab_end = N_EVEN_MAIN + 2 * H_B
    tail = [wt[:, ab_end:N_EVEN], wt[:, N_EVEN_MAIN:ab_end],
            jnp.zeros((wt.shape[0], N_EVEN_TAIL - (N_EVEN - N_EVEN_MAIN), wt.shape[2]), BF16)]
    return wt, jnp.concatenate(tail, axis=1)


def _trunk(x, caches, weights, tm_in, tm_out, tq, nsub, dn_sb, tt):
    (norm_gain, w_in_even, w_out_even, attn_sinks, dn_conv_w, dn_a_log, dn_dt_bias, dn_norm_gain,
     w_in_odd, w_out_odd, cf_conv_w, cf_conv_b, cf_ln_gain, cf_ln_bias, final_norm_gain) = weights
    nseq, t, _ = x.shape
    m = nseq * t
    x = x.reshape(m, D_MODEL)
    if caches is not None:
        cache_k, cache_v, state_dn, state_dn_conv, state_cf_conv = caches
        n_even = cache_k.shape[0]
        ck = cache_k.reshape(n_even, nseq * WINDOW, D_KV_A)
        cv = cache_v.reshape(n_even, nseq * WINDOW, D_KV_A)
    new_k, new_v, new_s, new_qkv, new_u = [], [], [], [], []
    for layer in range(DEPTH):
        i = layer // 2
        last = layer == DEPTH - 1
        if layer % 2 == 0:
            u, ut = rms_matmul(x, norm_gain[layer], w_in_even[0], N_EVEN_MAIN, w_in_even[1], i, tm_in, TN_EVEN)
            u3 = u.reshape(nseq, t, N_EVEN_MAIN)
            ut3 = ut.reshape(nseq, t, N_EVEN_TAIL)
            if caches is None:
                nw = t // WINDOW
                prev_spec = lambda col: pl.BlockSpec(
                    (WINDOW, D_KV_A), lambda n, j, col=col: (n * nw + jnp.maximum(nsub * j - 1, 0), col))
                o_a = attention(u, attn_sinks[i], u, u, prev_spec, nseq, t, tq, nsub, True)
                dn_state = None
            else:
                prev_spec = lambda col, i=i: pl.BlockSpec((None, nsub * WINDOW, D_KV_A), lambda n, j: (i, n, 0))
                o_a = attention(u, attn_sinks[i], ck, cv, prev_spec, nseq, t, tq, nsub, False)
                dn_state = (state_dn_conv, state_dn)
            o_b, s_new = deltanet(u3, ut3, dn_state, i, dn_conv_w[i], dn_a_log[i], dn_dt_bias[i], dn_norm_gain[i], dn_sb)
            x = matmul_residual([o_a, o_b.reshape(m, D_B)], w_out_even, i, x, tm_out)
            keep = min(t, WINDOW)
            heads = lambda z: z.reshape(z.shape[:2] + (N_KV_A, HD_A))
            new_k.append(heads(u3[:, t - keep:, COL_KA:COL_KA + D_KV_A]))
            new_v.append(heads(u3[:, t - keep:, COL_VA:COL_VA + D_KV_A]))
            new_s.append(s_new)
            new_qkv.append(u3[:, t - min(t, DN_CONV - 1):, COL_QKV_B:COL_QKV_B + D_QKV_B])
        else:
            uu, sz = rms_matmul_glu(x, norm_gain[layer], w_in_odd, i, tm_in, TN_ODD)
            prev_all = None if caches is None else state_cf_conv
            y = conformer_conv(uu, sz, prev_all, i, cf_conv_w[i], cf_conv_b[i], cf_ln_gain[i], cf_ln_bias[i],
                               nseq, t, tt)
            x = matmul_residual([y], w_out_odd, i, x, tm_out, final_norm_gain if last else None)
            new_u.append(uu.reshape(nseq, t, D_C)[:, t - min(t, CF_CONV - 1):])

    def with_history(old, new, length):
        new = jnp.stack(new)
        if new.shape[2] >= length:
            return new[:, :, new.shape[2] - length:]
        if old is None:
            old = jnp.zeros(new.shape[:2] + (length,) + new.shape[3:], new.dtype)
        return jnp.concatenate([old[:, :, new.shape[2]:], new], axis=2)

    old_k = old_v = old_dnc = old_cfc = None
    if caches is not None:
        old_k, old_v, old_dnc, old_cfc = cache_k, cache_v, state_dn_conv, state_cf_conv
    states = (with_history(old_k, new_k, WINDOW), with_history(old_v, new_v, WINDOW),
              jnp.stack(new_s), with_history(old_dnc, new_qkv, DN_CONV - 1),
              with_history(old_cfc, new_u, CF_CONV - 1))
    return x.reshape(nseq, t, D_MODEL), states


def kernel(x_prompt, x_sample, cache_win_k, cache_win_v, state_dn, state_dn_conv, state_cf_conv, norm_gain, w_in_even, w_out_even, attn_sinks, dn_conv_w, dn_a_log, dn_dt_bias, dn_norm_gain, w_in_odd, w_out_odd, cf_conv_w, cf_conv_b, cf_ln_gain, cf_ln_bias, final_norm_gain):
    weights = (norm_gain, _prep_even_weight(w_in_even), w_out_even.astype(BF16), attn_sinks, dn_conv_w, dn_a_log,
               dn_dt_bias, dn_norm_gain, w_in_odd.astype(BF16), w_out_odd.astype(BF16), cf_conv_w, cf_conv_b,
               cf_ln_gain, cf_ln_bias, final_norm_gain)
    y_prompt, (p_k, p_v, p_dn, p_dnc, p_cfc) = _trunk(
        x_prompt, None, weights, tm_in=1024, tm_out=512, tq=WINDOW, nsub=2, dn_sb=2, tt=512)
    caches = (cache_win_k, cache_win_v, state_dn, state_dn_conv, state_cf_conv)
    y_sample, (s_k, s_v, s_dn, s_dnc, s_cfc) = _trunk(
        x_sample, caches, weights, tm_in=256, tm_out=256, tq=x_sample.shape[1], nsub=4, dn_sb=4,
        tt=x_sample.shape[1])
    return (y_prompt, y_sample, p_k, p_v, p_dn, p_dnc, p_cfc, s_k, s_v, s_dn, s_dnc, s_cfc)
```

```python
import functools

import jax
import jax.numpy as jnp
from jax import lax
from jax.experimental import pallas as pl
from jax.experimental.pallas import tpu as pltpu

F32 = jnp.float32
BF16 = jnp.bfloat16

D_MODEL = 2048
DEPTH = 4
HD_A = 64
N_KV_A = 4
GROUP_A = 4
WINDOW = 128
D_A = 1024
D_KV_A = N_KV_A * HD_A
DK_B = 128
DV_B = 128
H_B = 8
D_B = 1024
D_QKV_B = 3072
DN_CONV = 4
DN_CHUNK = 64
D_C = 2048
CF_CONV = 31
EPS = 1e-6

COL_QA = 0
COL_KA = 1024
COL_VA = 1280
COL_ZA = 1536
COL_QKV_B = 2560
N_EVEN_MAIN = 5632
N_EVEN = 6672
TN_EVEN = 512
TAIL_ZB = 0
TAIL_AB = 1024
N_EVEN_TAIL = 1152
HALF = 512
TN_ODD = 512

LANES = 128
SUBLANES = 8
VMEM_LIMIT = 56 * 1024 * 1024


def _cparams(*sem):
    return pltpu.CompilerParams(dimension_semantics=sem, vmem_limit_bytes=VMEM_LIMIT)


def _act_dtype(block_rows):
    return BF16 if block_rows % 16 == 0 else F32


def _sigmoid(x):
    return 1.0 / (1.0 + jnp.exp(-x))


def _silu(x):
    return x * _sigmoid(x)


def _dot(a, b):
    return jnp.dot(a.astype(BF16), b.astype(BF16), preferred_element_type=F32)


def _dot_nt(a, b):
    return lax.dot_general(a.astype(BF16), b.astype(BF16), (((1,), (1,)), ((), ())),
                           preferred_element_type=F32)


def _dot_tn(a, b):
    return lax.dot_general(a.astype(BF16), b.astype(BF16), (((0,), (0,)), ((), ())),
                           preferred_element_type=F32)


def _split(a):
    hi = a.astype(BF16)
    lo = (a - hi.astype(F32)).astype(BF16)
    return hi, lo


NORM_ROWS = 256


def _normalize_rows(x_ref, g_ref, h_ref):
    tm = x_ref.shape[0]
    step = min(tm, NORM_ROWS)
    for r in range(0, tm, step):
        x = x_ref[r:r + step, :]
        ms = jnp.mean(x * x, axis=-1, keepdims=True)
        h_ref[r:r + step, :] = (x * lax.rsqrt(ms + EPS) * g_ref[...]).astype(BF16)


def _dot_wt(h, wt):
    return lax.dot_general(h, wt, (((1,), (1,)), ((), ())), preferred_element_type=F32)


def _rms_mm_body(x_ref, g_ref, wa_ref, wb_ref, oa_ref, ob_ref, h_ref, *, na):
    j = pl.program_id(1)

    @pl.when(j == 0)
    def _():
        _normalize_rows(x_ref, g_ref, h_ref)

    @pl.when(j < na)
    def _():
        oa_ref[...] = _dot_wt(h_ref[...], wa_ref[...])

    @pl.when(j == na)
    def _():
        ob_ref[...] = _dot_wt(h_ref[...], wb_ref[...])


def rms_matmul(x, g, wta_all, n_main, wtb_all, li, tm, tn):
    m, k = x.shape
    na = n_main // tn
    nb = wtb_all.shape[1]
    return pl.pallas_call(
        functools.partial(_rms_mm_body, na=na),
        grid=(m // tm, na + 1),
        in_specs=[
            pl.BlockSpec((tm, k), lambda i, j: (i, 0)),
            pl.BlockSpec((1, k), lambda i, j: (0, 0)),
            pl.BlockSpec((None, tn, k), lambda i, j: (li, jnp.minimum(j, na - 1), 0)),
            pl.BlockSpec((None, nb, k), lambda i, j: (li, 0, 0)),
        ],
        out_specs=[
            pl.BlockSpec((tm, tn), lambda i, j: (i, jnp.minimum(j, na - 1))),
            pl.BlockSpec((tm, nb), lambda i, j: (i, 0)),
        ],
        out_shape=[jax.ShapeDtypeStruct((m, n_main), F32), jax.ShapeDtypeStruct((m, nb), F32)],
        scratch_shapes=[pltpu.VMEM((tm, k), BF16)],
        compiler_params=_cparams("parallel", "arbitrary"),
        name="rms_matmul",
    )(x, g.reshape(1, k), wta_all, wtb_all)


def _rms_mm_glu_body(x_ref, g_ref, wv_ref, wg_ref, wz_ref, u_ref, sz_ref, h_ref):
    @pl.when(pl.program_id(1) == 0)
    def _():
        _normalize_rows(x_ref, g_ref, h_ref)

    h = h_ref[...]
    val = jnp.dot(h, wv_ref[...], preferred_element_type=F32)
    gate = jnp.dot(h, wg_ref[...], preferred_element_type=F32)
    z = jnp.dot(h, wz_ref[...], preferred_element_type=F32)
    u_ref[...] = val * _sigmoid(gate)
    sz_ref[...] = _silu(z)


def rms_matmul_glu(x, g, w_all, li, tm, tn):
    m, k = x.shape
    n = w_all.shape[2] // 3
    nt = n // tn
    return pl.pallas_call(
        _rms_mm_glu_body,
        grid=(m // tm, nt),
        in_specs=[
            pl.BlockSpec((tm, k), lambda i, j: (i, 0)),
            pl.BlockSpec((1, k), lambda i, j: (0, 0)),
            pl.BlockSpec((None, k, tn), lambda i, j: (li, 0, j)),
            pl.BlockSpec((None, k, tn), lambda i, j: (li, 0, j + nt)),
            pl.BlockSpec((None, k, tn), lambda i, j: (li, 0, j + 2 * nt)),
        ],
        out_specs=[
            pl.BlockSpec((tm, tn), lambda i, j: (i, j)),
            pl.BlockSpec((tm, tn), lambda i, j: (i, j)),
        ],
        out_shape=[jax.ShapeDtypeStruct((m, n), F32), jax.ShapeDtypeStruct((m, n), F32)],
        scratch_shapes=[pltpu.VMEM((tm, k), BF16)],
        compiler_params=_cparams("parallel", "arbitrary"),
        name="rms_matmul_glu",
    )(x, g.reshape(1, k), w_all, w_all, w_all)


def _mm_res_body(*refs, n_in, final):
    a_refs = refs[:n_in]
    w_refs = refs[n_in:2 * n_in]
    res_ref = refs[2 * n_in]
    o_ref = refs[-1]
    acc = res_ref[...]
    for a_ref, w_ref in zip(a_refs, w_refs):
        acc = acc + jnp.dot(a_ref[...].astype(BF16), w_ref[...], preferred_element_type=F32)
    if final:
        g_ref = refs[2 * n_in + 1]
        ms = jnp.mean(acc * acc, axis=-1, keepdims=True)
        acc = acc * lax.rsqrt(ms + EPS) * g_ref[...]
    o_ref[...] = acc


def matmul_residual(acts, w_all, li, res, tm, final_gain=None):
    m, n = res.shape
    n_in = len(acts)
    kb = acts[0].shape[1]
    assert all(a.shape[1] == kb for a in acts) and n_in * kb == w_all.shape[1]
    final = final_gain is not None
    in_specs = [pl.BlockSpec((tm, kb), lambda i: (i, 0)) for _ in acts]
    in_specs += [pl.BlockSpec((None, kb, n), lambda i, p=p: (li, p, 0)) for p in range(n_in)]
    in_specs += [pl.BlockSpec((tm, n), lambda i: (i, 0))]
    args = list(acts) + [w_all] * n_in + [res]
    if final:
        in_specs += [pl.BlockSpec((1, n), lambda i: (0, 0))]
        args += [final_gain.reshape(1, n)]
    return pl.pallas_call(
        functools.partial(_mm_res_body, n_in=n_in, final=final),
        grid=(m // tm,),
        in_specs=in_specs,
        out_specs=pl.BlockSpec((tm, n), lambda i: (i, 0)),
        out_shape=jax.ShapeDtypeStruct((m, n), F32),
        compiler_params=_cparams("parallel"),
        name="matmul_residual",
    )(*args)


def _attn_body(sink_ref, q_ref, za0_ref, za1_ref, kp_ref, kc_ref, vp_ref, vc_ref, o_ref, *, tq, nsub,
               first_block_has_no_prev):
    q = q_ref[...]
    za = jnp.concatenate([za0_ref[...], za1_ref[...]], axis=1)

    def keys(prev_ref, cur_ref):
        prev, cur = prev_ref[...], cur_ref[...]
        if tq == WINDOW:
            rows = jnp.concatenate([prev, cur], axis=0)
            return [rows[s * WINDOW:(s + 2) * WINDOW] for s in range(nsub)]
        pad = jnp.zeros((WINDOW - tq, D_KV_A), F32)
        return [jnp.concatenate([prev[s * WINDOW:(s + 1) * WINDOW], cur[s * tq:(s + 1) * tq], pad], axis=0)
                for s in range(nsub)]

    k = keys(kp_ref, kc_ref)
    v = keys(vp_ref, vc_ref)
    rows = GROUP_A * tq
    r = lax.broadcasted_iota(jnp.int32, (rows, 2 * WINDOW), 0)
    c = lax.broadcasted_iota(jnp.int32, (rows, 2 * WINDOW), 1)
    i = r & (tq - 1)
    mask = (c >= i) & (c <= i + WINDOW)
    masks = [mask] * nsub
    if first_block_has_no_prev:
        masks[0] = mask & ((c >= WINDOW) | (pl.program_id(1) > 0))
    ones_col = (lax.broadcasted_iota(jnp.int32, (2 * WINDOW, HD_A), 1) == 0).astype(F32)
    units = [(s, h) for s in range(nsub) for h in range(N_KV_A)]
    scores = []
    for s, h in units:
        base = h * GROUP_A * HD_A
        qs = jnp.concatenate([q[s * tq:(s + 1) * tq, base + g * HD_A: base + (g + 1) * HD_A]
                              for g in range(GROUP_A)], axis=0)
        scores.append(_dot_nt(qs, k[s][:, h * HD_A:(h + 1) * HD_A]))
    exps, sink_terms = [], []
    for (s, h), sc in zip(units, scores):
        sc = jnp.where(masks[s], sc * (HD_A ** -0.5), -jnp.inf)
        sk = jnp.concatenate([jnp.full((tq, 1), sink_ref[h * GROUP_A + g], F32) for g in range(GROUP_A)], axis=0)
        m = jnp.maximum(jnp.max(sc, axis=-1, keepdims=True), sk)
        exps.append(jnp.exp(sc - m).astype(BF16))
        sink_terms.append(jnp.exp(sk - m))
    outs = [[] for _ in range(nsub)]
    for (s, h), e, st in zip(units, exps, sink_terms):
        v_ext = jnp.concatenate([v[s][:, h * HD_A:(h + 1) * HD_A], ones_col], axis=1)
        pv = jnp.dot(e, v_ext.astype(BF16), preferred_element_type=F32)
        o = pv[:, :HD_A] / (pv[:, HD_A:HD_A + 1] + st)
        outs[s] += [o[g * tq:(g + 1) * tq] for g in range(GROUP_A)]
    o_all = jnp.concatenate([jnp.concatenate(o, axis=1) for o in outs], axis=0)
    o_ref[...] = (o_all * _silu(za)).astype(o_ref.dtype)


def attention(u, sinks, prev_k, prev_v, prev_spec, nseq, t, tq, nsub, first_block_has_no_prev):
    bq = tq * nsub
    nb = max(1, t // bq)
    groups = nseq * t // (bq * nb)
    row = lambda n, j: n * nb + j
    return pl.pallas_call(
        functools.partial(_attn_body, tq=tq, nsub=nsub, first_block_has_no_prev=first_block_has_no_prev),
        grid=(groups, nb),
        in_specs=[
            pl.BlockSpec(memory_space=pltpu.SMEM),
            pl.BlockSpec((bq, D_A), lambda n, j: (row(n, j), COL_QA // D_A)),
            pl.BlockSpec((bq, HALF), lambda n, j: (row(n, j), COL_ZA // HALF)),
            pl.BlockSpec((bq, HALF), lambda n, j: (row(n, j), COL_ZA // HALF + 1)),
            prev_spec(COL_KA // D_KV_A),
            pl.BlockSpec((bq, D_KV_A), lambda n, j: (row(n, j), COL_KA // D_KV_A)),
            prev_spec(COL_VA // D_KV_A),
            pl.BlockSpec((bq, D_KV_A), lambda n, j: (row(n, j), COL_VA // D_KV_A)),
        ],
        out_specs=pl.BlockSpec((bq, D_A), lambda n, j: (row(n, j), 0)),
        out_shape=jax.ShapeDtypeStruct((nseq * t, D_A), _act_dtype(bq)),
        compiler_params=_cparams("parallel", "arbitrary"),
        name="swa_attention",
    )(sinks, u, u, u, prev_k, u, prev_v, u)


DN_HALO = SUBLANES
DN_MIN_CHUNK = 16


def _softplus(x):
    return jnp.maximum(x, 0.0) + jnp.log1p(jnp.exp(-jnp.abs(x)))


def _deltanet_body(*refs, sb, tv, ch, nc, has_state):
    n_qkv = D_QKV_B // HALF
    qkv_refs = refs[:n_qkv]
    ab_ref, zb0_ref, zb1_ref = refs[n_qkv:n_qkv + 3]
    if has_state:
        cprev_ref, s0_ref = refs[n_qkv + 3:n_qkv + 5]
    cw_ref, alog_ref, dtb_ref, ng_ref, o_ref, sfin_ref, xp_ref, s_ref = refs[-8:]
    c = pl.program_id(1)
    halo = DN_HALO
    first_tap_row = halo - (DN_CONV - 1)

    @pl.when(c == 0)
    def _():
        if has_state:
            s_ref[...] = s0_ref[...]
            xp_ref[:, first_tap_row:halo, :] = cprev_ref[...]
        else:
            s_ref[...] = jnp.zeros(s_ref.shape, F32)
            xp_ref[:, 0:halo, :] = jnp.zeros((sb, halo, D_QKV_B), F32)

    ii = lax.broadcasted_iota(jnp.int32, (ch, ch), 0)
    jj = lax.broadcasted_iota(jnp.int32, (ch, ch), 1)
    lower = ii >= jj
    strict = ii > jj
    eye = (ii == jj).astype(F32)
    tril_ones = lower.astype(F32)
    sel_rows = (lax.broadcasted_iota(jnp.int32, (H_B, LANES), 0)
                == lax.broadcasted_iota(jnp.int32, (H_B, LANES), 1)).astype(F32)
    row_id = lax.broadcasted_iota(jnp.int32, (ch, 1), 0)
    cw = cw_ref[...]
    hp = lax.Precision.HIGHEST

    units = []
    for n in range(sb):
        for p, part_ref in enumerate(qkv_refs):
            xp_ref[n, halo:halo + tv, p * HALF:(p + 1) * HALF] = part_ref[n]
        if tv < ch:
            xp_ref[n, halo + tv:halo + ch, :] = jnp.zeros((ch - tv, D_QKV_B), F32)
        y = jnp.zeros((ch, D_QKV_B), F32)
        for tap in range(DN_CONV):
            off = first_tap_row + tap
            y = y + cw[tap:tap + 1, :] * xp_ref[n, off:off + ch, :]
        y = _silu(y)
        if tv < ch:
            y = jnp.where(row_id < tv, y, 0.0)
        if nc > 1:
            xp_ref[n, 0:halo, :] = xp_ref[n, tv:tv + halo, :]
        ab = ab_ref[n]
        if tv < ch:
            ab = jnp.concatenate([ab, jnp.zeros((ch - tv, LANES), F32)], axis=0)
        gpre = -jnp.exp(alog_ref[...]) * _softplus(ab + dtb_ref[...])
        if tv < ch:
            gpre = jnp.where(row_id < tv, gpre, 0.0)
        beta_all = _sigmoid(ab)
        gcum = jnp.dot(tril_ones, gpre, precision=hp, preferred_element_type=F32)
        gcum_t = lax.dot_general(sel_rows, gcum, (((1,), (1,)), ((), ())), precision=hp,
                                 preferred_element_type=F32)
        for h in range(H_B):
            qh = y[:, h * DK_B:(h + 1) * DK_B]
            kh = y[:, 1024 + h * DK_B:1024 + (h + 1) * DK_B]
            vh = y[:, 2048 + h * DV_B:2048 + (h + 1) * DV_B]
            qh = qh * lax.rsqrt(jnp.sum(qh * qh, axis=-1, keepdims=True) + EPS) * (DK_B ** -0.5)
            kh = kh * lax.rsqrt(jnp.sum(kh * kh, axis=-1, keepdims=True) + EPS)
            gc = gcum[:, h:h + 1]
            gr = gcum_t[h:h + 1, :]
            glast = gcum[tv - 1:tv, h:h + 1]
            beta = beta_all[:, H_B + h:H_B + h + 1]
            egc = jnp.exp(gc)
            kbeta = kh * beta
            units.append(dict(
                n=n, h=h, qh=qh, kh=kh, kbeta=kbeta, glast=glast,
                decay=jnp.where(lower, jnp.exp(jnp.where(lower, gc - gr, 0.0)), 0.0),
                rhs=jnp.concatenate([vh * beta, kbeta * egc], axis=1),
                qg=qh * egc, kg=kh * jnp.exp(glast - gc)))

    for u in units:
        kq = _dot_nt(jnp.concatenate([u["kbeta"], u["qh"]], axis=0), u["kh"])
        u["w"] = jnp.concatenate([-jnp.where(strict, kq[:ch] * u["decay"], 0.0), eye], axis=1)
        u["attn"] = kq[ch:] * u["decay"]

    right = lax.broadcasted_iota(jnp.int32, (ch, 2 * ch), 1) >= ch
    for _ in range(max(1, (tv - 1).bit_length())):
        prods = [_dot(u["w"][:, :ch], u["w"]) for u in units]
        for u, m in zip(units, prods):
            u["w"] = m + jnp.where(right, u["w"], 0.0)

    for u in units:
        hi, lo = _split(u["rhs"])
        sol = jnp.dot(u["w"][:, ch:].astype(BF16), jnp.concatenate([hi, lo], axis=1), preferred_element_type=F32)
        u["u"] = sol[:, 0:DV_B] + sol[:, 2 * DV_B:3 * DV_B]
        u["wk"] = sol[:, DV_B:2 * DV_B] + sol[:, 3 * DV_B:4 * DV_B]

    for u in units:
        ws_qs = _dot(jnp.concatenate([u["wk"], u["qg"]], axis=0), s_ref[u["n"], u["h"]])
        u["v_new"] = u["u"] - ws_qs[:ch]
        u["o"] = ws_qs[ch:]

    for u in units:
        u["o"] = u["o"] + _dot(u["attn"], u["v_new"])
        s_ref[u["n"], u["h"]] = s_ref[u["n"], u["h"]] * jnp.exp(u["glast"]) + _dot_tn(u["kg"], u["v_new"])

    for n in range(sb):
        outs = []
        for u in units[n * H_B:(n + 1) * H_B]:
            o = u["o"]
            outs.append(o * lax.rsqrt(jnp.mean(o * o, axis=-1, keepdims=True) + EPS) * ng_ref[...])
        o_all = jnp.concatenate(outs, axis=1)[:tv]
        zb = jnp.concatenate([zb0_ref[n], zb1_ref[n]], axis=1)
        o_ref[n] = (o_all * _silu(zb)).astype(o_ref.dtype)

    @pl.when(c == nc - 1)
    def _():
        sfin_ref[...] = s_ref[...]


def deltanet(u3, ut3, state, li, conv_w, a_log, dt_bias, norm_g, sb):
    nseq, t, _ = u3.shape
    tv = min(DN_CHUNK, t)
    nc = t // tv
    ch = DN_CHUNK if tv == DN_CHUNK else max(tv, DN_MIN_CHUNK)
    has_state = state is not None
    pad8 = lambda x: jnp.zeros((1, LANES), F32).at[0, :H_B].set(x)
    blk = lambda col: pl.BlockSpec((sb, tv, HALF), lambda s, c: (s, c, col))
    const = lambda shape: pl.BlockSpec(shape, lambda s, c: tuple(0 for _ in shape))
    in_specs = [blk(COL_QKV_B // HALF + p) for p in range(D_QKV_B // HALF)]
    args = [u3] * len(in_specs)
    in_specs += [pl.BlockSpec((sb, tv, LANES), lambda s, c: (s, c, TAIL_AB // LANES)),
                 blk(TAIL_ZB // HALF), blk(TAIL_ZB // HALF + 1)]
    args += [ut3] * 3
    if has_state:
        in_specs += [
            pl.BlockSpec((None, sb, DN_CONV - 1, D_QKV_B), lambda s, c: (li, s, 0, 0)),
            pl.BlockSpec((None, sb, H_B, DK_B, DV_B), lambda s, c: (li, s, 0, 0, 0)),
        ]
        args += list(state)
    in_specs += [const((DN_CONV, D_QKV_B)), const((1, LANES)), const((1, LANES)), const((1, DV_B))]
    args += [conv_w, pad8(a_log), pad8(dt_bias), norm_g.reshape(1, DV_B)]
    return pl.pallas_call(
        functools.partial(_deltanet_body, sb=sb, tv=tv, ch=ch, nc=nc, has_state=has_state),
        grid=(nseq // sb, nc),
        in_specs=in_specs,
        out_specs=[
            pl.BlockSpec((sb, tv, D_B), lambda s, c: (s, c, 0)),
            pl.BlockSpec((sb, H_B, DK_B, DV_B), lambda s, c: (s, 0, 0, 0)),
        ],
        out_shape=[jax.ShapeDtypeStruct((nseq, t, D_B), _act_dtype(tv)),
                   jax.ShapeDtypeStruct((nseq, H_B, DK_B, DV_B), F32)],
        scratch_shapes=[pltpu.VMEM((sb, DN_HALO + ch, D_QKV_B), F32),
                        pltpu.VMEM((sb, H_B, DK_B, DV_B), F32)],
        compiler_params=_cparams("parallel", "arbitrary"),
        name="gated_deltanet",
    )(*args)


CF_HALO = 32
CF_FIRST = CF_HALO - (CF_CONV - 1)
CF_GROUPS_AHEAD = (CF_HALO + SUBLANES) // SUBLANES


def _tree_sum(terms):
    while len(terms) > 1:
        terms = [terms[i] + terms[i + 1] for i in range(0, len(terms) - 1, 2)] + terms[len(terms) & ~1:]
    return terms[0]


def _group_rows(g):
    if isinstance(g, int):
        return pl.ds(g * SUBLANES, SUBLANES)
    return pl.ds(pl.multiple_of(g * SUBLANES, SUBLANES), SUBLANES)


def _cfconv_body(*refs, tt, has_prev):
    if has_prev:
        u_ref, sz_ref, prev_ref, cw_ref, cb_ref, lg_ref, lb_ref, o_ref, xp_ref, wb_ref, acc_ref = refs
    else:
        u_ref, sz_ref, cw_ref, cb_ref, lg_ref, lb_ref, o_ref, xp_ref, wb_ref, acc_ref = refs

    @pl.when((pl.program_id(0) == 0) & (pl.program_id(1) == 0))
    def _():
        for tap in range(CF_CONV):
            wb_ref[tap] = jnp.broadcast_to(cw_ref[tap:tap + 1, :], (SUBLANES, D_C))

    @pl.when(pl.program_id(1) == 0)
    def _():
        xp_ref[0:CF_HALO, :] = jnp.zeros((CF_HALO, D_C), F32)
        if has_prev:
            xp_ref[CF_FIRST:CF_HALO, :] = prev_ref[...]
        xp_ref[CF_HALO + tt:CF_HALO + tt + SUBLANES, :] = jnp.zeros((SUBLANES, D_C), F32)

    xp_ref[CF_HALO:CF_HALO + tt, :] = u_ref[...]

    row = lax.broadcasted_iota(jnp.int32, (SUBLANES, LANES), 0)
    ngrp = tt // SUBLANES
    for ci in range(D_C // LANES):
        cols = slice(ci * LANES, (ci + 1) * LANES)

        def zgroup(g, cols=cols):
            xs = [xp_ref[_group_rows(g + a), cols] for a in range(CF_GROUPS_AHEAD)]
            zs = []
            for b in range(SUBLANES):
                terms = [wb_ref[SUBLANES * a + b - CF_FIRST, :, cols] * xs[a]
                         for a in range(CF_GROUPS_AHEAD) if 0 <= SUBLANES * a + b - CF_FIRST < CF_CONV]
                zs.append(_tree_sum(terms))
            return tuple(zs)

        def body(g, zprev, cols=cols, zgroup=zgroup):
            znext = zgroup(g + 1)
            terms = [zprev[0]] + [pltpu.roll(jnp.where(row < b, znext[b], zprev[b]), SUBLANES - b, axis=0)
                                  for b in range(1, SUBLANES)]
            acc_ref[_group_rows(g), cols] = _tree_sum(terms)
            return znext

        lax.fori_loop(0, ngrp, body, zgroup(0), unroll=4 if ngrp % 4 == 0 else 1)

    rows = 2 * SUBLANES if tt % (2 * SUBLANES) == 0 else SUBLANES

    def norm_rows(g, carry):
        sl = pl.ds(pl.multiple_of(g * rows, rows), rows)
        acc = acc_ref[sl, :] + cb_ref[...]
        xc = acc - jnp.mean(acc, axis=-1, keepdims=True)
        y = xc * lax.rsqrt(jnp.mean(xc * xc, axis=-1, keepdims=True) + EPS)
        y = y * lg_ref[...] + lb_ref[...]
        o_ref[sl, :] = (_silu(y) * sz_ref[sl, :]).astype(o_ref.dtype)
        return carry

    ngrp_norm = tt // rows
    lax.fori_loop(0, ngrp_norm, norm_rows, 0, unroll=4 if ngrp_norm % 4 == 0 else 1)
    if tt >= CF_HALO:
        xp_ref[0:CF_HALO, :] = xp_ref[tt:tt + CF_HALO, :]


def conformer_conv(u, sz, prev_all, li, conv_w, conv_b, ln_g, ln_b, nseq, t, tt):
    nb = t // tt
    assert nb == 1 or tt >= CF_HALO
    has_prev = prev_all is not None
    row = lambda n, j: (n * nb + j, 0)
    const = lambda shape: pl.BlockSpec(shape, lambda n, j: (0, 0))
    in_specs = [pl.BlockSpec((tt, D_C), row), pl.BlockSpec((tt, D_C), row)]
    args = [u, sz]
    if has_prev:
        in_specs += [pl.BlockSpec((None, None, CF_CONV - 1, D_C), lambda n, j: (li, n, 0, 0))]
        args += [prev_all]
    in_specs += [const((CF_CONV, D_C)), const((1, D_C)), const((1, D_C)), const((1, D_C))]
    args += [conv_w, conv_b.reshape(1, D_C), ln_g.reshape(1, D_C), ln_b.reshape(1, D_C)]
    return pl.pallas_call(
        functools.partial(_cfconv_body, tt=tt, has_prev=has_prev),
        grid=(nseq, nb),
        in_specs=in_specs,
        out_specs=pl.BlockSpec((tt, D_C), row),
        out_shape=jax.ShapeDtypeStruct((nseq * t, D_C), _act_dtype(tt)),
        scratch_shapes=[pltpu.VMEM((CF_HALO + tt + SUBLANES, D_C), F32),
                        pltpu.VMEM((CF_CONV, SUBLANES, D_C), F32),
                        pltpu.VMEM((tt, D_C), F32)],
        compiler_params=_cparams("arbitrary", "arbitrary"),
        name="conformer_conv",
    )(*args)


def _prep_even_weight(w):
    wt = jnp.swapaxes(w, 1, 2).astype(BF16)
    ab_end = N_EVEN_MAIN + 2 * H_B
    tail = [wt[:, ab_end:N_EVEN], wt[:, N_EVEN_MAIN:ab_end],
            jnp.zeros((wt.shape[0], N_EVEN_TAIL - (N_EVEN - N_EVEN_MAIN), wt.shape[2]), BF16)]
    return wt, jnp.concatenate(tail, axis=1)


def _trunk(x, caches, weights, tm_in, tm_out, tq, nsub, dn_sb, tt):
    (norm_gain, w_in_even, w_out_even, attn_sinks, dn_conv_w, dn_a_log, dn_dt_bias, dn_norm_gain,
     w_in_odd, w_out_odd, cf_conv_w, cf_conv_b, cf_ln_gain, cf_ln_bias, final_norm_gain) = weights
    nseq, t, _ = x.shape
    m = nseq * t
    x = x.reshape(m, D_MODEL)
    if caches is not None:
        cache_k, cache_v, state_dn, state_dn_conv, state_cf_conv = caches
        n_even = cache_k.shape[0]
        ck = cache_k.reshape(n_even, nseq * WINDOW, D_KV_A)
        cv = cache_v.reshape(n_even, nseq * WINDOW, D_KV_A)
    new_k, new_v, new_s, new_qkv, new_u = [], [], [], [], []
    for layer in range(DEPTH):
        i = layer // 2
        last = layer == DEPTH - 1
        if layer % 2 == 0:
            u, ut = rms_matmul(x, norm_gain[layer], w_in_even[0], N_EVEN_MAIN, w_in_even[1], i, tm_in, TN_EVEN)
            u3 = u.reshape(nseq, t, N_EVEN_MAIN)
            ut3 = ut.reshape(nseq, t, N_EVEN_TAIL)
            if caches is None:
                nw = t // WINDOW
                prev_spec = lambda col: pl.BlockSpec(
                    (WINDOW, D_KV_A), lambda n, j, col=col: (n * nw + jnp.maximum(nsub * j - 1, 0), col))
                o_a = attention(u, attn_sinks[i], u, u, prev_spec, nseq, t, tq, nsub, True)
                dn_state = None
            else:
                prev_spec = lambda col, i=i: pl.BlockSpec((None, nsub * WINDOW, D_KV_A), lambda n, j: (i, n, 0))
                o_a = attention(u, attn_sinks[i], ck, cv, prev_spec, nseq, t, tq, nsub, False)
                dn_state = (state_dn_conv, state_dn)
            o_b, s_new = deltanet(u3, ut3, dn_state, i, dn_conv_w[i], dn_a_log[i], dn_dt_bias[i], dn_norm_gain[i], dn_sb)
            x = matmul_residual([o_a, o_b.reshape(m, D_B)], w_out_even, i, x, tm_out)
            keep = min(t, WINDOW)
            heads = lambda z: z.reshape(z.shape[:2] + (N_KV_A, HD_A))
            new_k.append(heads(u3[:, t - keep:, COL_KA:COL_KA + D_KV_A]))
            new_v.append(heads(u3[:, t - keep:, COL_VA:COL_VA + D_KV_A]))
            new_s.append(s_new)
            new_qkv.append(u3[:, t - min(t, DN_CONV - 1):, COL_QKV_B:COL_QKV_B + D_QKV_B])
        else:
            uu, sz = rms_matmul_glu(x, norm_gain[layer], w_in_odd, i, tm_in, TN_ODD)
            prev_all = None if caches is None else state_cf_conv
            y = conformer_conv(uu, sz, prev_all, i, cf_conv_w[i], cf_conv_b[i], cf_ln_gain[i], cf_ln_bias[i],
                               nseq, t, tt)
            x = matmul_residual([y], w_out_odd, i, x, tm_out, final_norm_gain if last else None)
            new_u.append(uu.reshape(nseq, t, D_C)[:, t - min(t, CF_CONV - 1):])

    def with_history(old, new, length):
        new = jnp.stack(new)
        if new.shape[2] >= length:
            return new[:, :, new.shape[2] - length:]
        if old is None:
            old = jnp.zeros(new.shape[:2] + (length,) + new.shape[3:], new.dtype)
        return jnp.concatenate([old[:, :, new.shape[2]:], new], axis=2)

    old_k = old_v = old_dnc = old_cfc = None
    if caches is not None:
        old_k, old_v, old_dnc, old_cfc = cache_k, cache_v, state_dn_conv, state_cf_conv
    states = (with_history(old_k, new_k, WINDOW), with_history(old_v, new_v, WINDOW),
              jnp.stack(new_s), with_history(old_dnc, new_qkv, DN_CONV - 1),
              with_history(old_cfc, new_u, CF_CONV - 1))
    return x.reshape(nseq, t, D_MODEL), states


def kernel(x_prompt, x_sample, cache_win_k, cache_win_v, state_dn, state_dn_conv, state_cf_conv, norm_gain, w_in_even, w_out_even, attn_sinks, dn_conv_w, dn_a_log, dn_dt_bias, dn_norm_gain, w_in_odd, w_out_odd, cf_conv_w, cf_conv_b, cf_ln_gain, cf_ln_bias, final_norm_gain):
    weights = (norm_gain, _prep_even_weight(w_in_even), w_out_even.astype(BF16), attn_sinks, dn_conv_w, dn_a_log,
               dn_dt_bias, dn_norm_gain, w_in_odd.astype(BF16), w_out_odd.astype(BF16), cf_conv_w, cf_conv_b,
               cf_ln_gain, cf_ln_bias, final_norm_gain)
    y_prompt, (p_k, p_v, p_dn, p_dnc, p_cfc) = _trunk(
        x_prompt, None, weights, tm_in=1024, tm_out=512, tq=WINDOW, nsub=2, dn_sb=2, tt=512)
    caches = (cache_win_k, cache_win_v, state_dn, state_dn_conv, state_cf_conv)
    y_sample, (s_k, s_v, s_dn, s_dnc, s_cfc) = _trunk(
        x_sample, caches, weights, tm_in=256, tm_out=256, tq=x_sample.shape[1], nsub=8, dn_sb=8,
        tt=x_sample.shape[1])
    return (y_prompt, y_sample, p_k, p_v, p_dn, p_dnc, p_cfc, s_k, s_v, s_dn, s_dnc, s_cfc)
```
